```python
import jax, jax.numpy as jnp
from jax import lax
import numpy as np

D_MODEL = 1024
BATCH = 1
SEQ = 16384
DEPTH = 2
DEC_BATCH = 32
DEC_SEQ = 2048
PAST_LEN = 128

MIX_WIDTH = D_MODEL
REC_WIDTH = MIX_WIDTH // 2
N_REC_BLOCKS = 8
REC_BLOCK = REC_WIDTH // N_REC_BLOCKS
CONV_WIDTH = 4
LRU_C = 8.0
HEAD_DIM = 64
N_Q_HEADS = (MIX_WIDTH - REC_WIDTH) // HEAD_DIM
N_KV_HEADS = 2
Q_PER_KV = N_Q_HEADS // N_KV_HEADS
ATT_WIDTH = N_Q_HEADS * HEAD_DIM
KV_WIDTH = N_KV_HEADS * HEAD_DIM
ROT_DIM = HEAD_DIM // 4
ROPE_THETA = 500000.0
WINDOW = 128
BLOCK = 128
D_FF = 3584
N_EXPERTS = 8
TOP_K = 2
EPS = 1e-6
N_DENSE = (DEPTH + 1) // 2
N_MOE = DEPTH // 2
IN_COLS = 2 * REC_WIDTH + ATT_WIDTH + 2 * KV_WIDTH
NEG_INF = -1e30

kernel_name = "hymba_rglru_swa_hybrid_encoder"


def rmsnorm(x, g):
    xf = x.astype(jnp.float32)
    y = xf * lax.rsqrt(jnp.mean(xf * xf, axis=-1, keepdims=True) + EPS) * g.astype(jnp.float32)
    return y.astype(x.dtype)


def partial_rope(x, pos):
    inv_freq = ROPE_THETA ** (-jnp.arange(0, ROT_DIM, 2, dtype=jnp.float32) / ROT_DIM)
    ang = pos.astype(jnp.float32)[:, None] * inv_freq[None, :]
    cos = jnp.cos(ang)[:, None, :]
    sin = jnp.sin(ang)[:, None, :]
    xf = x.astype(jnp.float32)
    x1 = xf[..., : ROT_DIM // 2]
    x2 = xf[..., ROT_DIM // 2: ROT_DIM]
    out = jnp.concatenate([x1 * cos - x2 * sin, x2 * cos + x1 * sin, xf[..., ROT_DIM:]], axis=-1)
    return out.astype(x.dtype)


def banded_gqa_attention(q, k, v, sink):
    B, S = q.shape[0], q.shape[1]
    nb = S // BLOCK
    qb = q.reshape(B, nb, BLOCK, N_KV_HEADS, Q_PER_KV, HEAD_DIM)

    def neighbour_blocks(t):
        tp = jnp.pad(t, ((0, 0), (BLOCK, BLOCK), (0, 0), (0, 0)))
        tb = tp.reshape(B, nb + 2, BLOCK, N_KV_HEADS, HEAD_DIM)
        return jnp.concatenate([tb[:, :-2], tb[:, 1:-1], tb[:, 2:]], axis=2)

    kb = neighbour_blocks(k)
    vb = neighbour_blocks(v)
    scale = HEAD_DIM ** -0.5
    s = jnp.einsum("bnqkgd,bnskd->bnkgqs", qb, kb, preferred_element_type=jnp.float32) * scale
    qpos = jnp.arange(nb)[:, None] * BLOCK + jnp.arange(BLOCK)[None, :]
    kpos = (jnp.arange(nb)[:, None] - 1) * BLOCK + jnp.arange(3 * BLOCK)[None, :]
    rel = kpos[:, None, :] - qpos[:, :, None]
    valid = (jnp.abs(rel) <= WINDOW) & (kpos[:, None, :] >= 0) & (kpos[:, None, :] < S)
    s = jnp.where(valid[None, :, None, None], s, NEG_INF)
    sinkf = sink.astype(jnp.float32).reshape(N_KV_HEADS, Q_PER_KV)[None, None, :, :, None, None]
    m = jnp.maximum(jnp.max(s, axis=-1, keepdims=True), sinkf)
    p = jnp.exp(s - m)
    denom = jnp.sum(p, axis=-1, keepdims=True) + jnp.exp(sinkf - m)
    probs = (p / denom).astype(v.dtype)
    o = jnp.einsum("bnkgqs,bnskd->bnqkgd", probs, vb)
    return o.reshape(B, S, ATT_WIDTH)


def _lin_combine(left, right):
    a1, b1 = left
    a2, b2 = right
    return a1 * a2, a2 * b1 + b2


def rglru_bidirectional(x, w_rgate, b_rgate, w_igate, b_igate, lam):
    B, S, _ = x.shape
    xf = x.astype(jnp.float32)
    xb = xf.reshape(B, S, N_REC_BLOCKS, REC_BLOCK)
    r = jax.nn.sigmoid(jnp.einsum("bsnc,dncf->dbsnf", xb, w_rgate.astype(jnp.float32)).reshape(2, B, S, REC_WIDTH)
                       + b_rgate.astype(jnp.float32)[:, None, None, :])
    i = jax.nn.sigmoid(jnp.einsum("bsnc,dncf->dbsnf", xb, w_igate.astype(jnp.float32)).reshape(2, B, S, REC_WIDTH)
                       + b_igate.astype(jnp.float32)[:, None, None, :])
    log_a = -LRU_C * r * jax.nn.softplus(-lam.astype(jnp.float32))[:, None, None, :]
    a = jnp.exp(log_a)
    u = jnp.sqrt(-jnp.expm1(2.0 * log_a)) * (i * xf[None])
    _, h_fwd = lax.associative_scan(_lin_combine, (a[0], u[0]), axis=1)
    _, h_bwd = lax.associative_scan(_lin_combine, (a[1], u[1]), axis=1, reverse=True)
    return h_fwd + h_bwd


def hybrid_mixer(h, w_in, conv_w, conv_b, w_rgate, b_rgate, w_igate, b_igate, lam, sink,
                 g_rec_out, g_attn_out, w_out):
    B, S, _ = h.shape
    proj = h @ w_in
    x_rec, gate, q, k, v = jnp.split(
        proj, [REC_WIDTH, 2 * REC_WIDTH, 2 * REC_WIDTH + ATT_WIDTH, 2 * REC_WIDTH + ATT_WIDTH + KV_WIDTH], axis=-1)
    left = CONV_WIDTH // 2
    xp = jnp.pad(x_rec, ((0, 0), (left, CONV_WIDTH - 1 - left), (0, 0)))
    xc = sum((xp[:, t:t + S] * conv_w[t] for t in range(CONV_WIDTH)), conv_b)
    h_rec = rglru_bidirectional(xc, w_rgate, b_rgate, w_igate, b_igate, lam).astype(h.dtype)
    y_rec = h_rec * jax.nn.gelu(gate, approximate=True)
    pos = jnp.arange(S)
    q = partial_rope(q.reshape(B, S, N_Q_HEADS, HEAD_DIM), pos)
    k = partial_rope(k.reshape(B, S, N_KV_HEADS, HEAD_DIM), pos)
    v = v.reshape(B, S, N_KV_HEADS, HEAD_DIM)
    y_att = banded_gqa_attention(q, k, v, sink)
    y = jnp.concatenate([rmsnorm(y_rec, g_rec_out), rmsnorm(y_att, g_attn_out)], axis=-1)
    return y @ w_out


def swiglu(h, w1, w3, w2):
    return (jax.nn.silu(h @ w1) * (h @ w3)) @ w2


def moe_swiglu(h, w_router, b_router, w1, w3, w2):
    B, S, D = h.shape
    t = h.reshape(B * S, D)
    logits = (t @ w_router).astype(jnp.float32) + b_router.astype(jnp.float32)
    top_val, top_idx = lax.top_k(logits, TOP_K)
    gates = jax.nn.softmax(top_val, axis=-1)
    combine = jnp.sum(jax.nn.one_hot(top_idx, N_EXPERTS, dtype=jnp.float32) * gates[..., None], axis=1)
    combine = combine.astype(t.dtype)
    out = jnp.zeros_like(t)
    for e in range(N_EXPERTS):
        he = jax.nn.silu(t @ w1[e]) * (t @ w3[e])
        out = out + combine[:, e:e + 1] * (he @ w2[e])
    return out.reshape(B, S, D)


def trunk(x, norm_mix, w_in, conv_w, conv_b, w_rgate, b_rgate, w_igate, b_igate, lru_lambda,
          attn_sink, norm_rec_out, norm_attn_out, w_out, norm_ffn, ffn_w1, ffn_w3, ffn_w2,
          moe_router, moe_router_bias, moe_w1, moe_w3, moe_w2, norm_final):
    h = x
    for l in range(DEPTH):
        h = h + hybrid_mixer(rmsnorm(h, norm_mix[l]), w_in[l], conv_w[l], conv_b[l], w_rgate[l], b_rgate[l],
                             w_igate[l], b_igate[l], lru_lambda[l], attn_sink[l], norm_rec_out[l],
                             norm_attn_out[l], w_out[l])
        hn = rmsnorm(h, norm_ffn[l])
        j = l // 2
        if l % 2 == 0:
            h = h + swiglu(hn, ffn_w1[j], ffn_w3[j], ffn_w2[j])
        else:
            h = h + moe_swiglu(hn, moe_router[j], moe_router_bias[j], moe_w1[j], moe_w3[j], moe_w2[j])
    return rmsnorm(h, norm_final)


def setup_inputs(seed: int = 0) -> dict:
    key = jax.random.key(seed)
    ks = iter(jax.random.split(key, 32))
    f32 = jnp.float32

    def nrm(shape, scale):
        return jax.random.normal(next(ks), shape, f32) * scale

    def gain(shape):
        return 1.0 + nrm(shape, 0.02)

    x_prompt = jax.random.normal(next(ks), (BATCH, SEQ, D_MODEL), f32)
    x_sample = jax.random.normal(next(ks), (DEC_BATCH, DEC_SEQ, D_MODEL), f32)
    a_c = jax.random.uniform(next(ks), (DEPTH, 2, REC_WIDTH), f32, 0.9, 0.999)
    a0 = a_c ** (1.0 / LRU_C)
    lru_lambda = jnp.log(a0) - jnp.log1p(-a0)
    return {
        "x_prompt": x_prompt,
        "x_sample": x_sample,
        "norm_mix": gain((DEPTH, D_MODEL)),
        "w_in": nrm((DEPTH, D_MODEL, IN_COLS), D_MODEL ** -0.5),
        "conv_w": nrm((DEPTH, CONV_WIDTH, REC_WIDTH), CONV_WIDTH ** -0.5),
        "conv_b": nrm((DEPTH, REC_WIDTH), 0.02),
        "w_rgate": nrm((DEPTH, 2, N_REC_BLOCKS, REC_BLOCK, REC_BLOCK), REC_BLOCK ** -0.5),
        "b_rgate": nrm((DEPTH, 2, REC_WIDTH), 0.02),
        "w_igate": nrm((DEPTH, 2, N_REC_BLOCKS, REC_BLOCK, REC_BLOCK), REC_BLOCK ** -0.5),
        "b_igate": nrm((DEPTH, 2, REC_WIDTH), 0.02),
        "lru_lambda": lru_lambda,
        "attn_sink": nrm((DEPTH, N_Q_HEADS), 0.5),
        "norm_rec_out": gain((DEPTH, REC_WIDTH)),
        "norm_attn_out": gain((DEPTH, ATT_WIDTH)),
        "w_out": nrm((DEPTH, MIX_WIDTH, D_MODEL), MIX_WIDTH ** -0.5),
        "norm_ffn": gain((DEPTH, D_MODEL)),
        "ffn_w1": nrm((N_DENSE, D_MODEL, D_FF), D_MODEL ** -0.5),
        "ffn_w3": nrm((N_DENSE, D_MODEL, D_FF), D_MODEL ** -0.5),
        "ffn_w2": nrm((N_DENSE, D_FF, D_MODEL), D_FF ** -0.5),
        "moe_router": nrm((N_MOE, D_MODEL, N_EXPERTS), D_MODEL ** -0.5),
        "moe_router_bias": nrm((N_MOE, N_EXPERTS), 0.01),
        "moe_w1": nrm((N_MOE, N_EXPERTS, D_MODEL, D_FF), D_MODEL ** -0.5),
        "moe_w3": nrm((N_MOE, N_EXPERTS, D_MODEL, D_FF), D_MODEL ** -0.5),
        "moe_w2": nrm((N_MOE, N_EXPERTS, D_FF, D_MODEL), D_FF ** -0.5),
        "norm_final": gain((D_MODEL,)),
    }


def reference(x_prompt, x_sample, norm_mix, w_in, conv_w, conv_b, w_rgate, b_rgate, w_igate, b_igate,
              lru_lambda, attn_sink, norm_rec_out, norm_attn_out, w_out, norm_ffn, ffn_w1, ffn_w3, ffn_w2,
              moe_router, moe_router_bias, moe_w1, moe_w3, moe_w2, norm_final):
    y_prompt = trunk(x_prompt, norm_mix, w_in, conv_w, conv_b, w_rgate, b_rgate, w_igate, b_igate, lru_lambda,
                     attn_sink, norm_rec_out, norm_attn_out, w_out, norm_ffn, ffn_w1, ffn_w3, ffn_w2,
                     moe_router, moe_router_bias, moe_w1, moe_w3, moe_w2, norm_final)
    y_sample = trunk(x_sample, norm_mix, w_in, conv_w, conv_b, w_rgate, b_rgate, w_igate, b_igate, lru_lambda,
                     attn_sink, norm_rec_out, norm_attn_out, w_out, norm_ffn, ffn_w1, ffn_w3, ffn_w2,
                     moe_router, moe_router_bias, moe_w1, moe_w3, moe_w2, norm_final)
    return (y_prompt, y_sample)
```

```python
import functools

import jax
import jax.numpy as jnp
from jax import lax
from jax.experimental import pallas as pl
from jax.experimental.pallas import tpu as pltpu

F32 = jnp.float32
BF16 = jnp.bfloat16

D_MODEL = 1024
REC_WIDTH = 512
N_REC_BLOCKS = 8
REC_BLOCK = 64
CONV_WIDTH = 4
LRU_C = 8.0
HEAD_DIM = 64
N_Q_HEADS = 8
N_KV_HEADS = 2
Q_PER_KV = 4
ATT_WIDTH = 512
KV_WIDTH = 128
ROT_DIM = 16
ROPE_THETA = 500000.0
WINDOW = 128
D_FF = 3584
N_EXPERTS = 8
EPS = 1e-6
IN_COLS = 2 * REC_WIDTH + ATT_WIDTH + 2 * KV_WIDTH
NEG_INF = -1e30

LANES = 128
SUBLANES = 8
REC_CHUNKS = REC_WIDTH // LANES
HALO_ROWS = 16
VMEM_LIMIT = 56 * 1024 * 1024

ROW_BLOCK = 512
ATT_BLOCK = 512
FFN_ROWS = 1024
FF_CHUNK = 512


class Stream:
    def __init__(self, n_long, s_long, n_short, s_short):
        self.n_long_rows = n_long * s_long
        self.s_long = s_long
        self.s_short = s_short
        self.rows = n_long * s_long + n_short * s_short

    def seq_pos(self, row0):
        in_long = row0 < self.n_long_rows
        pos = jnp.where(in_long, lax.rem(row0, self.s_long),
                        lax.rem(jnp.maximum(row0 - self.n_long_rows, 0), self.s_short))
        slen = jnp.where(in_long, self.s_long, self.s_short)
        return pos, slen


def _params(*sem):
    return pltpu.CompilerParams(dimension_semantics=sem, vmem_limit_bytes=VMEM_LIMIT)


def _rms(x, g):
    return x * lax.rsqrt(jnp.mean(x * x, axis=-1, keepdims=True) + EPS) * g


def _inproj_kernel(x_ref, g_ref, w_ref, c_ref, sa_ref, sb_ref,
                   xrec_ref, gate_ref, q_ref, k_ref, v_ref):
    xn = _rms(x_ref[...], g_ref[...]).astype(BF16)
    proj = jnp.dot(xn, w_ref[...], preferred_element_type=F32)
    xrec_ref[...] = proj[:, :REC_WIDTH].astype(BF16)
    gate_ref[...] = proj[:, REC_WIDTH:2 * REC_WIDTH].astype(BF16)
    c, sa, sb = c_ref[...], sa_ref[...], sb_ref[...]

    def rope(t):
        return (t * c + pltpu.roll(t, LANES - ROT_DIM // 2, 1) * sa
                + pltpu.roll(t, ROT_DIM // 2, 1) * sb)

    q0 = 2 * REC_WIDTH
    scale = HEAD_DIM ** -0.5
    for j in range(ATT_WIDTH // LANES):
        t = proj[:, q0 + j * LANES:q0 + (j + 1) * LANES]
        q_ref[:, j * LANES:(j + 1) * LANES] = (rope(t) * scale).astype(BF16)
    k0 = q0 + ATT_WIDTH
    k_ref[...] = rope(proj[:, k0:k0 + KV_WIDTH]).astype(BF16)
    v_ref[...] = proj[:, k0 + KV_WIDTH:k0 + 2 * KV_WIDTH].astype(BF16)


def _rope_tables(s_max):
    inv_freq = ROPE_THETA ** (-jnp.arange(0, ROT_DIM, 2, dtype=F32) / ROT_DIM)
    ang = jnp.arange(s_max, dtype=F32)[:, None] * inv_freq[None, :]
    cos, sin = jnp.cos(ang), jnp.sin(ang)
    half = ROT_DIM // 2
    ones = jnp.ones((s_max, HEAD_DIM - ROT_DIM), F32)
    zeros_h = jnp.zeros((s_max, half), F32)
    zeros_r = jnp.zeros((s_max, HEAD_DIM - ROT_DIM), F32)
    c = jnp.concatenate([cos, cos, ones], axis=1)
    sa = jnp.concatenate([-sin, zeros_h, zeros_r], axis=1)
    sb = jnp.concatenate([zeros_h, sin, zeros_r], axis=1)
    reps = LANES // HEAD_DIM
    return tuple(jnp.tile(t, (1, reps)) for t in (c, sa, sb))


def _inproj(st, h, g, w_in, tables):
    R = ROW_BLOCK
    nblk = st.rows // R

    def pos_blk(i):
        pos, _ = st.seq_pos(i * R)
        return pos // R

    row = lambda i: (i, 0)
    const = lambda i: (0, 0)
    tab = pl.BlockSpec((R, LANES), lambda i: (pos_blk(i), 0))
    outs = [(REC_WIDTH, BF16), (REC_WIDTH, BF16), (ATT_WIDTH, BF16), (KV_WIDTH, BF16), (KV_WIDTH, BF16)]
    return pl.pallas_call(
        _inproj_kernel,
        grid=(nblk,),
        in_specs=[pl.BlockSpec((R, D_MODEL), row), pl.BlockSpec((1, D_MODEL), const),
                  pl.BlockSpec((D_MODEL, IN_COLS), const), tab, tab, tab],
        out_specs=[pl.BlockSpec((R, w), row) for w, _ in outs],
        out_shape=[jax.ShapeDtypeStruct((st.rows, w), dt) for w, dt in outs],
        compiler_params=_params("parallel"),
        name="inproj",
    )(h, g, w_in, *tables)


def _softplus(x):
    return jnp.maximum(x, 0.0) + jnp.log1p(jnp.exp(-jnp.abs(x)))


def _conv_gates(st, blk, x_ref, xp_ref, xn_ref, cw_ref, cb_ref, wg_ref, bg_ref, lam_ref, a_s, u_s):
    R = ROW_BLOCK
    pos, slen = st.seq_pos(blk * R)
    has_prev = (pos > 0).astype(F32)
    has_next = (pos + R < slen).astype(F32)
    x = x_ref[...].astype(F32)
    xe = jnp.concatenate([xp_ref[...].astype(F32) * has_prev, x, xn_ref[...].astype(F32) * has_next], axis=0)
    left = CONV_WIDTH // 2
    xc = cb_ref[...]
    for t in range(CONV_WIDTH):
        off = HALO_ROWS + t - left
        xc = xc + xe[off:off + R] * cw_ref[t:t + 1, :]
    pre = jnp.dot(xc.astype(BF16), wg_ref[...], preferred_element_type=F32) + bg_ref[...]
    r = jax.nn.sigmoid(pre[:, :REC_WIDTH])
    i = jax.nn.sigmoid(pre[:, REC_WIDTH:])
    log_a = (-LRU_C) * r * _softplus(-lam_ref[...])
    a = jnp.exp(log_a)
    u = jnp.sqrt(1.0 - a * a) * (i * xc)
    for c in range(REC_CHUNKS):
        a_s[c] = a[:, c * LANES:(c + 1) * LANES]
        u_s[c] = u[:, c * LANES:(c + 1) * LANES]


def _scan_block(a_s, u_s, hl_s, p_s, carry_in, reverse):
    seg = a_s.shape[1] // SUBLANES

    def step(t, hp):
        tt = seg - 1 - t if reverse else t
        idx = pl.ds(tt, SUBLANES, stride=seg)
        out = []
        for c in range(REC_CHUNKS):
            h, p = hp[c]
            a = a_s[c, idx, :]
            h = a * h + u_s[c, idx, :]
            p = a * p
            hl_s[c, idx, :] = h
            p_s[c, idx, :] = p
            out.append((h, p))
        return tuple(out)

    init = tuple((jnp.zeros((SUBLANES, LANES), F32), jnp.ones((SUBLANES, LANES), F32))
                 for _ in range(REC_CHUNKS))
    ends = lax.fori_loop(0, seg, step, init, unroll=4)
    enters, carry_out = [], []
    for c in range(REC_CHUNKS):
        h_end, p_end = ends[c]
        enter = [None] * SUBLANES
        cur = carry_in[c]
        for s in (range(SUBLANES - 1, -1, -1) if reverse else range(SUBLANES)):
            enter[s] = cur
            cur = h_end[s:s + 1] + p_end[s:s + 1] * cur
        enters.append(jnp.concatenate(enter, axis=0))
        carry_out.append(cur)
    return enters, carry_out


def _scan_apply(hl_s, p_s, enters, out_ref):
    seg = hl_s.shape[1] // SUBLANES

    def step(t, _):
        idx = pl.ds(t, SUBLANES, stride=seg)
        for c in range(REC_CHUNKS):
            out_ref[c, idx, :] = hl_s[c, idx, :] + p_s[c, idx, :] * enters[c]
        return 0

    lax.fori_loop(0, seg, step, 0, unroll=4)


def _rec_fwd_kernel(st, x_ref, xp_ref, xn_ref, cw_ref, cb_ref, wg_ref, bg_ref, lam_ref,
                    hf_ref, a_s, u_s, hl_s, p_s, carry_s):
    blk = pl.program_id(0)
    _conv_gates(st, blk, x_ref, xp_ref, xn_ref, cw_ref, cb_ref, wg_ref, bg_ref, lam_ref, a_s, u_s)
    pos, _ = st.seq_pos(blk * ROW_BLOCK)
    carry_in = [jnp.where(pos > 0, carry_s[c], 0.0) for c in range(REC_CHUNKS)]
    enters, carry_out = _scan_block(a_s, u_s, hl_s, p_s, carry_in, reverse=False)
    for c in range(REC_CHUNKS):
        carry_s[c] = carry_out[c]
    _scan_apply(hl_s, p_s, enters, hf_ref)


def _rec_bwd_kernel(st, nblk, x_ref, xp_ref, xn_ref, cw_ref, cb_ref, wg_ref, bg_ref, lam_ref,
                    hf_ref, gate_ref, gout_ref, y_ref, a_s, u_s, hl_s, p_s, hb_s, carry_s):
    blk = nblk - 1 - pl.program_id(0)
    _conv_gates(st, blk, x_ref, xp_ref, xn_ref, cw_ref, cb_ref, wg_ref, bg_ref, lam_ref, a_s, u_s)
    pos, slen = st.seq_pos(blk * ROW_BLOCK)
    carry_in = [jnp.where(pos + ROW_BLOCK < slen, carry_s[c], 0.0) for c in range(REC_CHUNKS)]
    enters, carry_out = _scan_block(a_s, u_s, hl_s, p_s, carry_in, reverse=True)
    for c in range(REC_CHUNKS):
        carry_s[c] = carry_out[c]
    _scan_apply(hl_s, p_s, enters, hb_s)
    ys = []
    for c in range(REC_CHUNKS):
        gate = gate_ref[:, c * LANES:(c + 1) * LANES].astype(F32)
        ys.append((hf_ref[c] + hb_s[c]) * jax.nn.gelu(gate, approximate=True))
    ms = sum(jnp.sum(y * y, axis=-1, keepdims=True) for y in ys) * (1.0 / REC_WIDTH)
    inv = lax.rsqrt(ms + EPS)
    for c in range(REC_CHUNKS):
        y_ref[:, c * LANES:(c + 1) * LANES] = (ys[c] * inv * gout_ref[:, c * LANES:(c + 1) * LANES]).astype(BF16)


def _recurrent(st, xrec, gate, conv_w, conv_b, wg_f, bg_f, wg_b, bg_b, lam_f, lam_b, g_rec_out):
    R = ROW_BLOCK
    nblk = st.rows // R
    hpb = R // HALO_ROWS
    n_halo = st.rows // HALO_ROWS
    const = lambda i: (0, 0)

    def specs(order):
        cur = lambda i: (order(i), 0)
        prev = lambda i: (jnp.maximum(order(i) * hpb - 1, 0), 0)
        nxt = lambda i: (jnp.minimum((order(i) + 1) * hpb, n_halo - 1), 0)
        return [pl.BlockSpec((R, REC_WIDTH), cur), pl.BlockSpec((HALO_ROWS, REC_WIDTH), prev),
                pl.BlockSpec((HALO_ROWS, REC_WIDTH), nxt),
                pl.BlockSpec((CONV_WIDTH, REC_WIDTH), const), pl.BlockSpec((1, REC_WIDTH), const),
                pl.BlockSpec((REC_WIDTH, 2 * REC_WIDTH), const), pl.BlockSpec((1, 2 * REC_WIDTH), const),
                pl.BlockSpec((1, REC_WIDTH), const)], cur

    chunked = (REC_CHUNKS, R, LANES)
    work = [pltpu.VMEM(chunked, F32) for _ in range(4)]
    carry = pltpu.VMEM((REC_CHUNKS, 1, LANES), F32)
    in_f, cur_f = specs(lambda i: i)
    hf = pl.pallas_call(
        functools.partial(_rec_fwd_kernel, st),
        grid=(nblk,),
        in_specs=in_f,
        out_specs=pl.BlockSpec(chunked, lambda i: (0, i, 0)),
        out_shape=jax.ShapeDtypeStruct((REC_CHUNKS, st.rows, LANES), F32),
        scratch_shapes=work + [carry],
        compiler_params=_params("arbitrary"),
        name="rec_fwd",
    )(xrec, xrec, xrec, conv_w, conv_b, wg_f, bg_f, lam_f)
    in_b, cur_b = specs(lambda i: nblk - 1 - i)
    return pl.pallas_call(
        functools.partial(_rec_bwd_kernel, st, nblk),
        grid=(nblk,),
        in_specs=in_b + [pl.BlockSpec(chunked, lambda i: (0, nblk - 1 - i, 0)), pl.BlockSpec((R, REC_WIDTH), cur_b),
                         pl.BlockSpec((1, REC_WIDTH), const)],
        out_specs=pl.BlockSpec((R, REC_WIDTH), cur_b),
        out_shape=jax.ShapeDtypeStruct((st.rows, REC_WIDTH), BF16),
        scratch_shapes=work + [pltpu.VMEM(chunked, F32), carry],
        compiler_params=_params("arbitrary"),
        name="rec_bwd",
    )(xrec, xrec, xrec, conv_w, conv_b, wg_b, bg_b, lam_b, hf, gate, g_rec_out)


def _attn_kernel(st, sink_ref, q_ref, k_ref, kp_ref, kn_ref, v_ref, vp_ref, vn_ref, g_ref,
                 y_ref, acc_s):
    QB = ATT_BLOCK
    W = WINDOW
    pos0, slen = st.seq_pos(pl.program_id(0) * QB)
    kx = jnp.concatenate([kp_ref[...], k_ref[...], kn_ref[...]], axis=0)
    vx = jnp.concatenate([vp_ref[...], v_ref[...], vn_ref[...]], axis=0)
    qi = lax.broadcasted_iota(jnp.int32, (W, 3 * W), 0)
    kj = lax.broadcasted_iota(jnp.int32, (W, 3 * W), 1)
    band = jnp.abs(kj - W - qi) <= W
    for b in range(QB // W):
        kpos = pos0 + (b - 1) * W + kj
        valid = band & (kpos >= 0) & (kpos < slen)
        valid = jnp.concatenate([valid] * Q_PER_KV, axis=0)
        for g in range(N_KV_HEADS):
            heads = range(g * Q_PER_KV, (g + 1) * Q_PER_KV)
            q4 = jnp.concatenate(
                [q_ref[b * W:(b + 1) * W, h * HEAD_DIM:(h + 1) * HEAD_DIM] for h in heads], axis=0)
            kg = kx[b * W:(b + 3) * W, g * HEAD_DIM:(g + 1) * HEAD_DIM]
            vg = vx[b * W:(b + 3) * W, g * HEAD_DIM:(g + 1) * HEAD_DIM]
            s = lax.dot_general(q4, kg, (((1,), (1,)), ((), ())), preferred_element_type=F32)
            s = jnp.where(valid, s, NEG_INF)
            sink = jnp.concatenate([jnp.full((W, 1), sink_ref[h], F32) for h in heads], axis=0)
            m = jnp.maximum(jnp.max(s, axis=-1, keepdims=True), sink)
            p = jnp.exp(s - m)
            denom = jnp.sum(p, axis=-1, keepdims=True) + jnp.exp(sink - m)
            o = jnp.dot(p.astype(BF16), vg, preferred_element_type=F32) / denom
            for n, h in enumerate(heads):
                acc_s[b * W:(b + 1) * W, h * HEAD_DIM:(h + 1) * HEAD_DIM] = o[n * W:(n + 1) * W]
    y_ref[...] = _rms(acc_s[...], g_ref[...]).astype(BF16)


def _attention(st, q, k, v, sink, g_attn_out):
    QB = ATT_BLOCK
    W = WINDOW
    nblk = st.rows // QB
    wpb = QB // W
    n_w = st.rows // W
    cur = lambda i, s: (i, 0)
    prev = lambda i, s: (jnp.maximum(i * wpb - 1, 0), 0)
    nxt = lambda i, s: (jnp.minimum((i + 1) * wpb, n_w - 1), 0)
    kv = [pl.BlockSpec((QB, KV_WIDTH), cur), pl.BlockSpec((W, KV_WIDTH), prev), pl.BlockSpec((W, KV_WIDTH), nxt)]
    return pl.pallas_call(
        functools.partial(_attn_kernel, st),
        grid_spec=pltpu.PrefetchScalarGridSpec(
            num_scalar_prefetch=1,
            grid=(nblk,),
            in_specs=[pl.BlockSpec((QB, ATT_WIDTH), cur)] + kv + kv
                     + [pl.BlockSpec((1, ATT_WIDTH), lambda i, s: (0, 0))],
            out_specs=pl.BlockSpec((QB, ATT_WIDTH), cur),
            scratch_shapes=[pltpu.VMEM((QB, ATT_WIDTH), F32)],
        ),
        out_shape=jax.ShapeDtypeStruct((st.rows, ATT_WIDTH), BF16),
        compiler_params=_params("parallel"),
        name="attention",
    )(sink, q, k, k, k, v, v, v, g_attn_out)


def _outproj_kernel(h_ref, yr_ref, ya_ref, w_ref, g_ref, hout_ref, hn_ref):
    y = (jnp.dot(yr_ref[...], w_ref[:REC_WIDTH, :], preferred_element_type=F32)
         + jnp.dot(ya_ref[...], w_ref[REC_WIDTH:, :], preferred_element_type=F32))
    h = h_ref[...] + y
    hout_ref[...] = h
    hn_ref[...] = _rms(h, g_ref[...]).astype(BF16)


def _outproj_router_kernel(h_ref, yr_ref, ya_ref, w_ref, g_ref, wr_ref, br_ref,
                           hout_ref, hn_ref, comb_ref):
    y = (jnp.dot(yr_ref[...], w_ref[:REC_WIDTH, :], preferred_element_type=F32)
         + jnp.dot(ya_ref[...], w_ref[REC_WIDTH:, :], preferred_element_type=F32))
    h = h_ref[...] + y
    hout_ref[...] = h
    hn = _rms(h, g_ref[...])
    hn_ref[...] = hn.astype(BF16)
    logits = jnp.dot(hn, wr_ref[...], preferred_element_type=F32,
                     precision=lax.Precision.HIGHEST) + br_ref[...]
    lane = lax.broadcasted_iota(jnp.int32, logits.shape, 1)
    logits = jnp.where(lane < N_EXPERTS, logits, -jnp.inf)
    m1 = jnp.max(logits, axis=-1, keepdims=True)
    i1 = jnp.min(jnp.where(logits == m1, lane, LANES), axis=-1, keepdims=True)
    rest = jnp.where(lane == i1, -jnp.inf, logits)
    m2 = jnp.max(rest, axis=-1, keepdims=True)
    i2 = jnp.min(jnp.where(rest == m2, lane, LANES), axis=-1, keepdims=True)
    e2 = jnp.exp(m2 - m1)
    g1 = 1.0 / (1.0 + e2)
    g2 = e2 * g1
    comb_ref[...] = jnp.where(lane == i1, g1, 0.0) + jnp.where(lane == i2, g2, 0.0)


def _outproj(st, h, yrec, yatt, w_out, g_ffn, router=None):
    R = ROW_BLOCK
    nblk = st.rows // R
    row = lambda i: (i, 0)
    const = lambda i: (0, 0)
    in_specs = [pl.BlockSpec((R, D_MODEL), row), pl.BlockSpec((R, REC_WIDTH), row),
                pl.BlockSpec((R, ATT_WIDTH), row), pl.BlockSpec((D_MODEL, D_MODEL), const),
                pl.BlockSpec((1, D_MODEL), const)]
    out_specs = [pl.BlockSpec((R, D_MODEL), row), pl.BlockSpec((R, D_MODEL), row)]
    out_shape = [jax.ShapeDtypeStruct((st.rows, D_MODEL), F32), jax.ShapeDtypeStruct((st.rows, D_MODEL), BF16)]
    args = [h, yrec, yatt, w_out, g_ffn]
    body = _outproj_kernel
    if router is not None:
        in_specs += [pl.BlockSpec((D_MODEL, LANES), const), pl.BlockSpec((1, LANES), const)]
        out_specs.append(pl.BlockSpec((R, LANES), row))
        out_shape.append(jax.ShapeDtypeStruct((st.rows, LANES), F32))
        args += list(router)
        body = _outproj_router_kernel
    return pl.pallas_call(
        body, grid=(nblk,), in_specs=in_specs, out_specs=out_specs, out_shape=out_shape,
        compiler_params=_params("parallel"), name="outproj",
    )(*args)


def _ffn_kernel(x_ref, h_ref, w1_ref, w3_ref, w2_ref, out_ref, acc_s):
    j = pl.program_id(1)
    x = x_ref[...]
    a = jnp.dot(x, w1_ref[...], preferred_element_type=F32)
    b = jnp.dot(x, w3_ref[...], preferred_element_type=F32)
    hh = (a * jax.nn.sigmoid(a) * b).astype(BF16)
    y = jnp.dot(hh, w2_ref[...], preferred_element_type=F32)

    @pl.when(j == 0)
    def _():
        acc_s[...] = h_ref[...] + y

    @pl.when(j > 0)
    def _():
        acc_s[...] += y

    @pl.when(j == pl.num_programs(1) - 1)
    def _():
        out_ref[...] = acc_s[...]


def _ffn_dense(st, hn, h, w1, w3, w2):
    RB, FC = FFN_ROWS, FF_CHUNK
    row = lambda i, j: (i, 0)
    return pl.pallas_call(
        _ffn_kernel,
        grid=(st.rows // RB, D_FF // FC),
        in_specs=[pl.BlockSpec((RB, D_MODEL), row), pl.BlockSpec((RB, D_MODEL), row),
                  pl.BlockSpec((D_MODEL, FC), lambda i, j: (0, j)),
                  pl.BlockSpec((D_MODEL, FC), lambda i, j: (0, j)),
                  pl.BlockSpec((FC, D_MODEL), lambda i, j: (j, 0))],
        out_specs=pl.BlockSpec((RB, D_MODEL), row),
        out_shape=jax.ShapeDtypeStruct((st.rows, D_MODEL), F32),
        scratch_shapes=[pltpu.VMEM((RB, D_MODEL), F32)],
        compiler_params=_params("parallel", "arbitrary"),
        name="ffn_dense",
    )(hn, h, w1, w3, w2)


def _moe_dense_kernel(x_ref, h_ref, comb_ref, w1_ref, w3_ref, w2_ref, g_ref, out_ref, acc_s):
    e = pl.program_id(1)
    j = pl.program_id(2)
    x = x_ref[...]
    comb = comb_ref[...]
    lane = lax.broadcasted_iota(jnp.int32, comb.shape, 1)
    cw = jnp.sum(jnp.where(lane == e, comb, 0.0), axis=-1, keepdims=True)
    a = jnp.dot(x, w1_ref[...], preferred_element_type=F32)
    b = jnp.dot(x, w3_ref[...], preferred_element_type=F32)
    hh = (a * jax.nn.sigmoid(a) * b * cw).astype(BF16)
    y = jnp.dot(hh, w2_ref[...], preferred_element_type=F32)
    first = (e == 0) & (j == 0)

    @pl.when(first)
    def _():
        acc_s[...] = h_ref[...] + y

    @pl.when(jnp.logical_not(first))
    def _():
        acc_s[...] += y

    @pl.when((e == pl.num_programs(1) - 1) & (j == pl.num_programs(2) - 1))
    def _():
        out_ref[...] = _rms(acc_s[...], g_ref[...])


def _moe_dense(st, hn, h, comb, w1, w3, w2, g_final):
    RB, FC = FFN_ROWS, FF_CHUNK
    row = lambda i, e, j: (i, 0)
    return pl.pallas_call(
        _moe_dense_kernel,
        grid=(st.rows // RB, N_EXPERTS, D_FF // FC),
        in_specs=[pl.BlockSpec((RB, D_MODEL), row), pl.BlockSpec((RB, D_MODEL), row),
                  pl.BlockSpec((RB, LANES), row),
                  pl.BlockSpec((None, D_MODEL, FC), lambda i, e, j: (e, 0, j)),
                  pl.BlockSpec((None, D_MODEL, FC), lambda i, e, j: (e, 0, j)),
                  pl.BlockSpec((None, FC, D_MODEL), lambda i, e, j: (e, j, 0)),
                  pl.BlockSpec((1, D_MODEL), lambda i, e, j: (0, 0))],
        out_specs=pl.BlockSpec((RB, D_MODEL), row),
        out_shape=jax.ShapeDtypeStruct((st.rows, D_MODEL), F32),
        scratch_shapes=[pltpu.VMEM((RB, D_MODEL), F32)],
        compiler_params=_params("parallel", "arbitrary", "arbitrary"),
        name="moe_dense",
    )(hn, h, comb, w1, w3, w2, g_final)


def _block_diag(w):
    nb, bs, _ = w.shape
    eye = jnp.eye(nb, dtype=w.dtype)
    return jnp.einsum("ncf,nm->ncmf", w, eye).reshape(nb * bs, nb * bs)


def kernel(x_prompt, x_sample, norm_mix, w_in, conv_w, conv_b, w_rgate, b_rgate, w_igate, b_igate,
           lru_lambda, attn_sink, norm_rec_out, norm_attn_out, w_out, norm_ffn, ffn_w1, ffn_w3, ffn_w2,
           moe_router, moe_router_bias, moe_w1, moe_w3, moe_w2, norm_final):
    n_long, s_long, _ = x_prompt.shape
    n_short, s_short, _ = x_sample.shape
    depth = w_in.shape[0]
    assert depth == 2, "layer 0 uses the dense feed-forward, layer 1 the mixture of experts"
    for s in (s_long, s_short):
        assert s % max(ROW_BLOCK, ATT_BLOCK) == 0
    st = Stream(n_long, s_long, n_short, s_short)
    assert st.rows % FFN_ROWS == 0

    h = jnp.concatenate([x_prompt.reshape(-1, D_MODEL), x_sample.reshape(-1, D_MODEL)], axis=0)
    tables = _rope_tables(max(s_long, s_short))
    row2 = lambda a: a.reshape(1, -1)

    out = None
    for l in range(depth):
        xrec, gate, q, k, v = _inproj(st, h, row2(norm_mix[l]), w_in[l].astype(BF16), tables)
        wg = [jnp.concatenate([_block_diag(w_rgate[l, d]), _block_diag(w_igate[l, d])], axis=1).astype(BF16)
              for d in range(2)]
        bg = [row2(jnp.concatenate([b_rgate[l, d], b_igate[l, d]])) for d in range(2)]
        yrec = _recurrent(st, xrec, gate, conv_w[l], row2(conv_b[l]), wg[0], bg[0], wg[1], bg[1],
                          row2(lru_lambda[l, 0]), row2(lru_lambda[l, 1]), row2(norm_rec_out[l]))
        yatt = _attention(st, q, k, v, attn_sink[l], row2(norm_attn_out[l]))
        if l % 2 == 0:
            j = l // 2
            h, hn = _outproj(st, h, yrec, yatt, w_out[l].astype(BF16), row2(norm_ffn[l]))
            h = _ffn_dense(st, hn, h, ffn_w1[j].astype(BF16), ffn_w3[j].astype(BF16), ffn_w2[j].astype(BF16))
        else:
            j = l // 2
            wr = jnp.zeros((D_MODEL, LANES), F32).at[:, :N_EXPERTS].set(moe_router[j])
            br = jnp.zeros((1, LANES), F32).at[0, :N_EXPERTS].set(moe_router_bias[j])
            h, hn, comb = _outproj(st, h, yrec, yatt, w_out[l].astype(BF16), row2(norm_ffn[l]), router=(wr, br))
            out = _moe_dense(st, hn, h, comb, moe_w1[j].astype(BF16), moe_w3[j].astype(BF16),
                             moe_w2[j].astype(BF16), row2(norm_final))
    y_prompt = out[:st.n_long_rows].reshape(x_prompt.shape)
    y_sample = out[st.n_long_rows:].reshape(x_sample.shape)
    return (y_prompt, y_sample)
```

```python
import functools

import jax
import jax.numpy as jnp
from jax import lax
from jax.experimental import pallas as pl
from jax.experimental.pallas import tpu as pltpu

F32 = jnp.float32
BF16 = jnp.bfloat16

D_MODEL = 1024
REC_WIDTH = 512
N_REC_BLOCKS = 8
REC_BLOCK = 64
CONV_WIDTH = 4
LRU_C = 8.0
HEAD_DIM = 64
N_Q_HEADS = 8
N_KV_HEADS = 2
Q_PER_KV = 4
ATT_WIDTH = 512
KV_WIDTH = 128
ROT_DIM = 16
ROPE_THETA = 500000.0
WINDOW = 128
D_FF = 3584
N_EXPERTS = 8
EPS = 1e-6
IN_COLS = 2 * REC_WIDTH + ATT_WIDTH + 2 * KV_WIDTH
NEG_INF = -1e30

LANES = 128
SUBLANES = 8
REC_CHUNKS = REC_WIDTH // LANES
HALO_ROWS = 16
VMEM_LIMIT = 56 * 1024 * 1024

ROW_BLOCK = 512
ATT_BLOCK = 512
FFN_ROWS = 1024
FF_CHUNK = 512
MOVE_ROWS = 1024
TOP_K = 2


class Stream:
    def __init__(self, n_long, s_long, n_short, s_short):
        self.n_long_rows = n_long * s_long
        self.s_long = s_long
        self.s_short = s_short
        self.rows = n_long * s_long + n_short * s_short

    def seq_pos(self, row0):
        in_long = row0 < self.n_long_rows
        pos = jnp.where(in_long, lax.rem(row0, self.s_long),
                        lax.rem(jnp.maximum(row0 - self.n_long_rows, 0), self.s_short))
        slen = jnp.where(in_long, self.s_long, self.s_short)
        return pos, slen


def _params(*sem):
    return pltpu.CompilerParams(dimension_semantics=sem, vmem_limit_bytes=VMEM_LIMIT)


def _rms(x, g):
    return x * lax.rsqrt(jnp.mean(x * x, axis=-1, keepdims=True) + EPS) * g


def _inproj_kernel(x_ref, g_ref, w_ref, c_ref, sa_ref, sb_ref,
                   xrec_ref, gate_ref, q_ref, k_ref, v_ref):
    xn = _rms(x_ref[...], g_ref[...]).astype(BF16)
    proj = jnp.dot(xn, w_ref[...], preferred_element_type=F32)
    xrec_ref[...] = proj[:, :REC_WIDTH].astype(BF16)
    gate_ref[...] = proj[:, REC_WIDTH:2 * REC_WIDTH].astype(BF16)
    c, sa, sb = c_ref[...], sa_ref[...], sb_ref[...]

    def rope(t):
        return (t * c + pltpu.roll(t, LANES - ROT_DIM // 2, 1) * sa
                + pltpu.roll(t, ROT_DIM // 2, 1) * sb)

    q0 = 2 * REC_WIDTH
    scale = HEAD_DIM ** -0.5
    for j in range(ATT_WIDTH // LANES):
        t = proj[:, q0 + j * LANES:q0 + (j + 1) * LANES]
        q_ref[:, j * LANES:(j + 1) * LANES] = (rope(t) * scale).astype(BF16)
    k0 = q0 + ATT_WIDTH
    k_ref[...] = rope(proj[:, k0:k0 + KV_WIDTH]).astype(BF16)
    v_ref[...] = proj[:, k0 + KV_WIDTH:k0 + 2 * KV_WIDTH].astype(BF16)


def _rope_tables(s_max):
    inv_freq = ROPE_THETA ** (-jnp.arange(0, ROT_DIM, 2, dtype=F32) / ROT_DIM)
    ang = jnp.arange(s_max, dtype=F32)[:, None] * inv_freq[None, :]
    cos, sin = jnp.cos(ang), jnp.sin(ang)
    half = ROT_DIM // 2
    ones = jnp.ones((s_max, HEAD_DIM - ROT_DIM), F32)
    zeros_h = jnp.zeros((s_max, half), F32)
    zeros_r = jnp.zeros((s_max, HEAD_DIM - ROT_DIM), F32)
    c = jnp.concatenate([cos, cos, ones], axis=1)
    sa = jnp.concatenate([-sin, zeros_h, zeros_r], axis=1)
    sb = jnp.concatenate([zeros_h, sin, zeros_r], axis=1)
    reps = LANES // HEAD_DIM
    return tuple(jnp.tile(t, (1, reps)) for t in (c, sa, sb))


def _inproj(st, h, g, w_in, tables):
    R = ROW_BLOCK
    nblk = st.rows // R

    def pos_blk(i):
        pos, _ = st.seq_pos(i * R)
        return pos // R

    row = lambda i: (i, 0)
    const = lambda i: (0, 0)
    tab = pl.BlockSpec((R, LANES), lambda i: (pos_blk(i), 0))
    outs = [(REC_WIDTH, BF16), (REC_WIDTH, BF16), (ATT_WIDTH, BF16), (KV_WIDTH, BF16), (KV_WIDTH, BF16)]
    return pl.pallas_call(
        _inproj_kernel,
        grid=(nblk,),
        in_specs=[pl.BlockSpec((R, D_MODEL), row), pl.BlockSpec((1, D_MODEL), const),
                  pl.BlockSpec((D_MODEL, IN_COLS), const), tab, tab, tab],
        out_specs=[pl.BlockSpec((R, w), row) for w, _ in outs],
        out_shape=[jax.ShapeDtypeStruct((st.rows, w), dt) for w, dt in outs],
        compiler_params=_params("parallel"),
        name="inproj",
    )(h, g, w_in, *tables)


def _softplus(x):
    return jnp.maximum(x, 0.0) + jnp.log1p(jnp.exp(-jnp.abs(x)))


def _conv_gates(st, blk, x_ref, xp_ref, xn_ref, cw_ref, cb_ref, wg_ref, bg_ref, lam_ref, a_s, u_s):
    R = ROW_BLOCK
    pos, slen = st.seq_pos(blk * R)
    has_prev = (pos > 0).astype(F32)
    has_next = (pos + R < slen).astype(F32)
    x = x_ref[...].astype(F32)
    xe = jnp.concatenate([xp_ref[...].astype(F32) * has_prev, x, xn_ref[...].astype(F32) * has_next], axis=0)
    left = CONV_WIDTH // 2
    xc = cb_ref[...]
    for t in range(CONV_WIDTH):
        off = HALO_ROWS + t - left
        xc = xc + xe[off:off + R] * cw_ref[t:t + 1, :]
    pre = jnp.dot(xc.astype(BF16), wg_ref[...], preferred_element_type=F32) + bg_ref[...]
    r = jax.nn.sigmoid(pre[:, :REC_WIDTH])
    i = jax.nn.sigmoid(pre[:, REC_WIDTH:])
    log_a = (-LRU_C) * r * _softplus(-lam_ref[...])
    a = jnp.exp(log_a)
    u = jnp.sqrt(1.0 - a * a) * (i * xc)
    for c in range(REC_CHUNKS):
        a_s[c] = a[:, c * LANES:(c + 1) * LANES]
        u_s[c] = u[:, c * LANES:(c + 1) * LANES]


def _scan_block(a_s, u_s, hl_s, p_s, carry_in, reverse):
    seg = a_s.shape[1] // SUBLANES

    def step(t, hp):
        tt = seg - 1 - t if reverse else t
        idx = pl.ds(tt, SUBLANES, stride=seg)
        out = []
        for c in range(REC_CHUNKS):
            h, p = hp[c]
            a = a_s[c, idx, :]
            h = a * h + u_s[c, idx, :]
            p = a * p
            hl_s[c, idx, :] = h
            p_s[c, idx, :] = p
            out.append((h, p))
        return tuple(out)

    init = tuple((jnp.zeros((SUBLANES, LANES), F32), jnp.ones((SUBLANES, LANES), F32))
                 for _ in range(REC_CHUNKS))
    ends = lax.fori_loop(0, seg, step, init, unroll=4)
    enters, carry_out = [], []
    for c in range(REC_CHUNKS):
        h_end, p_end = ends[c]
        enter = [None] * SUBLANES
        cur = carry_in[c]
        for s in (range(SUBLANES - 1, -1, -1) if reverse else range(SUBLANES)):
            enter[s] = cur
            cur = h_end[s:s + 1] + p_end[s:s + 1] * cur
        enters.append(jnp.concatenate(enter, axis=0))
        carry_out.append(cur)
    return enters, carry_out


def _scan_apply(hl_s, p_s, enters, out_ref):
    seg = hl_s.shape[1] // SUBLANES

    def step(t, _):
        idx = pl.ds(t, SUBLANES, stride=seg)
        for c in range(REC_CHUNKS):
            out_ref[c, idx, :] = hl_s[c, idx, :] + p_s[c, idx, :] * enters[c]
        return 0

    lax.fori_loop(0, seg, step, 0, unroll=4)


def _rec_fwd_kernel(st, x_ref, xp_ref, xn_ref, cw_ref, cb_ref, wg_ref, bg_ref, lam_ref,
                    hf_ref, a_s, u_s, hl_s, p_s, carry_s):
    blk = pl.program_id(0)
    _conv_gates(st, blk, x_ref, xp_ref, xn_ref, cw_ref, cb_ref, wg_ref, bg_ref, lam_ref, a_s, u_s)
    pos, _ = st.seq_pos(blk * ROW_BLOCK)
    carry_in = [jnp.where(pos > 0, carry_s[c], 0.0) for c in range(REC_CHUNKS)]
    enters, carry_out = _scan_block(a_s, u_s, hl_s, p_s, carry_in, reverse=False)
    for c in range(REC_CHUNKS):
        carry_s[c] = carry_out[c]
    _scan_apply(hl_s, p_s, enters, hf_ref)


def _rec_bwd_kernel(st, nblk, x_ref, xp_ref, xn_ref, cw_ref, cb_ref, wg_ref, bg_ref, lam_ref,
                    hf_ref, gate_ref, gout_ref, y_ref, a_s, u_s, hl_s, p_s, hb_s, carry_s):
    blk = nblk - 1 - pl.program_id(0)
    _conv_gates(st, blk, x_ref, xp_ref, xn_ref, cw_ref, cb_ref, wg_ref, bg_ref, lam_ref, a_s, u_s)
    pos, slen = st.seq_pos(blk * ROW_BLOCK)
    carry_in = [jnp.where(pos + ROW_BLOCK < slen, carry_s[c], 0.0) for c in range(REC_CHUNKS)]
    enters, carry_out = _scan_block(a_s, u_s, hl_s, p_s, carry_in, reverse=True)
    for c in range(REC_CHUNKS):
        carry_s[c] = carry_out[c]
    _scan_apply(hl_s, p_s, enters, hb_s)
    ys = []
    for c in range(REC_CHUNKS):
        gate = gate_ref[:, c * LANES:(c + 1) * LANES].astype(F32)
        ys.append((hf_ref[c] + hb_s[c]) * jax.nn.gelu(gate, approximate=True))
    ms = sum(jnp.sum(y * y, axis=-1, keepdims=True) for y in ys) * (1.0 / REC_WIDTH)
    inv = lax.rsqrt(ms + EPS)
    for c in range(REC_CHUNKS):
        y_ref[:, c * LANES:(c + 1) * LANES] = (ys[c] * inv * gout_ref[:, c * LANES:(c + 1) * LANES]).astype(BF16)


def _recurrent(st, xrec, gate, conv_w, conv_b, wg_f, bg_f, wg_b, bg_b, lam_f, lam_b, g_rec_out):
    R = ROW_BLOCK
    nblk = st.rows // R
    hpb = R // HALO_ROWS
    n_halo = st.rows // HALO_ROWS
    const = lambda i: (0, 0)

    def specs(order):
        cur = lambda i: (order(i), 0)
        prev = lambda i: (jnp.maximum(order(i) * hpb - 1, 0), 0)
        nxt = lambda i: (jnp.minimum((order(i) + 1) * hpb, n_halo - 1), 0)
        return [pl.BlockSpec((R, REC_WIDTH), cur), pl.BlockSpec((HALO_ROWS, REC_WIDTH), prev),
                pl.BlockSpec((HALO_ROWS, REC_WIDTH), nxt),
                pl.BlockSpec((CONV_WIDTH, REC_WIDTH), const), pl.BlockSpec((1, REC_WIDTH), const),
                pl.BlockSpec((REC_WIDTH, 2 * REC_WIDTH), const), pl.BlockSpec((1, 2 * REC_WIDTH), const),
                pl.BlockSpec((1, REC_WIDTH), const)], cur

    chunked = (REC_CHUNKS, R, LANES)
    work = [pltpu.VMEM(chunked, F32) for _ in range(4)]
    carry = pltpu.VMEM((REC_CHUNKS, 1, LANES), F32)
    in_f, cur_f = specs(lambda i: i)
    hf = pl.pallas_call(
        functools.partial(_rec_fwd_kernel, st),
        grid=(nblk,),
        in_specs=in_f,
        out_specs=pl.BlockSpec(chunked, lambda i: (0, i, 0)),
        out_shape=jax.ShapeDtypeStruct((REC_CHUNKS, st.rows, LANES), F32),
        scratch_shapes=work + [carry],
        compiler_params=_params("arbitrary"),
        name="rec_fwd",
    )(xrec, xrec, xrec, conv_w, conv_b, wg_f, bg_f, lam_f)
    in_b, cur_b = specs(lambda i: nblk - 1 - i)
    return pl.pallas_call(
        functools.partial(_rec_bwd_kernel, st, nblk),
        grid=(nblk,),
        in_specs=in_b + [pl.BlockSpec(chunked, lambda i: (0, nblk - 1 - i, 0)), pl.BlockSpec((R, REC_WIDTH), cur_b),
                         pl.BlockSpec((1, REC_WIDTH), const)],
        out_specs=pl.BlockSpec((R, REC_WIDTH), cur_b),
        out_shape=jax.ShapeDtypeStruct((st.rows, REC_WIDTH), BF16),
        scratch_shapes=work + [pltpu.VMEM(chunked, F32), carry],
        compiler_params=_params("arbitrary"),
        name="rec_bwd",
    )(xrec, xrec, xrec, conv_w, conv_b, wg_b, bg_b, lam_b, hf, gate, g_rec_out)


def _attn_kernel(st, sink_ref, q_ref, k_ref, kp_ref, kn_ref, v_ref, vp_ref, vn_ref, g_ref,
                 y_ref, acc_s):
    QB = ATT_BLOCK
    W = WINDOW
    pos0, slen = st.seq_pos(pl.program_id(0) * QB)
    kx = jnp.concatenate([kp_ref[...], k_ref[...], kn_ref[...]], axis=0)
    vx = jnp.concatenate([vp_ref[...], v_ref[...], vn_ref[...]], axis=0)
    qi = lax.broadcasted_iota(jnp.int32, (W, 3 * W), 0)
    kj = lax.broadcasted_iota(jnp.int32, (W, 3 * W), 1)
    band = jnp.abs(kj - W - qi) <= W
    for b in range(QB // W):
        kpos = pos0 + (b - 1) * W + kj
        valid = band & (kpos >= 0) & (kpos < slen)
        valid = jnp.concatenate([valid] * Q_PER_KV, axis=0)
        for g in range(N_KV_HEADS):
            heads = range(g * Q_PER_KV, (g + 1) * Q_PER_KV)
            q4 = jnp.concatenate(
                [q_ref[b * W:(b + 1) * W, h * HEAD_DIM:(h + 1) * HEAD_DIM] for h in heads], axis=0)
            kg = kx[b * W:(b + 3) * W, g * HEAD_DIM:(g + 1) * HEAD_DIM]
            vg = vx[b * W:(b + 3) * W, g * HEAD_DIM:(g + 1) * HEAD_DIM]
            s = lax.dot_general(q4, kg, (((1,), (1,)), ((), ())), preferred_element_type=F32)
            s = jnp.where(valid, s, NEG_INF)
            sink = jnp.concatenate([jnp.full((W, 1), sink_ref[h], F32) for h in heads], axis=0)
            m = jnp.maximum(jnp.max(s, axis=-1, keepdims=True), sink)
            p = jnp.exp(s - m)
            denom = jnp.sum(p, axis=-1, keepdims=True) + jnp.exp(sink - m)
            o = jnp.dot(p.astype(BF16), vg, preferred_element_type=F32) / denom
            for n, h in enumerate(heads):
                acc_s[b * W:(b + 1) * W, h * HEAD_DIM:(h + 1) * HEAD_DIM] = o[n * W:(n + 1) * W]
    y_ref[...] = _rms(acc_s[...], g_ref[...]).astype(BF16)


def _attention(st, q, k, v, sink, g_attn_out):
    QB = ATT_BLOCK
    W = WINDOW
    nblk = st.rows // QB
    wpb = QB // W
    n_w = st.rows // W
    cur = lambda i, s: (i, 0)
    prev = lambda i, s: (jnp.maximum(i * wpb - 1, 0), 0)
    nxt = lambda i, s: (jnp.minimum((i + 1) * wpb, n_w - 1), 0)
    kv = [pl.BlockSpec((QB, KV_WIDTH), cur), pl.BlockSpec((W, KV_WIDTH), prev), pl.BlockSpec((W, KV_WIDTH), nxt)]
    return pl.pallas_call(
        functools.partial(_attn_kernel, st),
        grid_spec=pltpu.PrefetchScalarGridSpec(
            num_scalar_prefetch=1,
            grid=(nblk,),
            in_specs=[pl.BlockSpec((QB, ATT_WIDTH), cur)] + kv + kv
                     + [pl.BlockSpec((1, ATT_WIDTH), lambda i, s: (0, 0))],
            out_specs=pl.BlockSpec((QB, ATT_WIDTH), cur),
            scratch_shapes=[pltpu.VMEM((QB, ATT_WIDTH), F32)],
        ),
        out_shape=jax.ShapeDtypeStruct((st.rows, ATT_WIDTH), BF16),
        compiler_params=_params("parallel"),
        name="attention",
    )(sink, q, k, k, k, v, v, v, g_attn_out)


def _outproj_kernel(h_ref, yr_ref, ya_ref, w_ref, g_ref, hout_ref, hn_ref):
    y = (jnp.dot(yr_ref[...], w_ref[:REC_WIDTH, :], preferred_element_type=F32)
         + jnp.dot(ya_ref[...], w_ref[REC_WIDTH:, :], preferred_element_type=F32))
    h = h_ref[...] + y
    hout_ref[...] = h
    hn_ref[...] = _rms(h, g_ref[...]).astype(BF16)


def _pack_halves(x):
    n = x.shape[1] // 2

    def bf16_bits(v):
        b = lax.bitcast_convert_type(v, jnp.uint32)
        return (b + jnp.uint32(0x7FFF) + ((b >> 16) & jnp.uint32(1))) >> 16

    return bf16_bits(x[:, :n]) | (bf16_bits(x[:, n:]) << 16)


def _unpack_halves(w):
    lo = lax.bitcast_convert_type(w << 16, F32)
    hi = lax.bitcast_convert_type(w & jnp.uint32(0xFFFF0000), F32)
    return lo, hi


META_E1, META_E2, META_R1, META_R2, META_G1, META_G2 = range(6)


def _outproj_router_kernel(h_ref, yr_ref, ya_ref, w_ref, g_ref, wr_ref, br_ref, tri_ref,
                           hout_ref, hnp_ref, meta_ref, metat_ref, counts_ref, carry_s):
    @pl.when(pl.program_id(0) == 0)
    def _():
        carry_s[...] = jnp.zeros_like(carry_s)

    y = (jnp.dot(yr_ref[...], w_ref[:REC_WIDTH, :], preferred_element_type=F32)
         + jnp.dot(ya_ref[...], w_ref[REC_WIDTH:, :], preferred_element_type=F32))
    h = h_ref[...] + y
    hout_ref[...] = h
    hn = _rms(h, g_ref[...])
    hnp_ref[...] = _pack_halves(hn)
    hi = hn.astype(BF16)
    lo = (hn - hi.astype(F32)).astype(BF16)
    a = jnp.dot(hi, wr_ref[...], preferred_element_type=F32)
    b = jnp.dot(lo, wr_ref[...], preferred_element_type=F32)
    logits = a + pltpu.roll(a, LANES - N_EXPERTS, 1) + b + br_ref[...]
    lane = lax.broadcasted_iota(jnp.int32, logits.shape, 1)
    logits = jnp.where(lane < N_EXPERTS, logits, -jnp.inf)
    m1 = jnp.max(logits, axis=-1, keepdims=True)
    i1 = jnp.min(jnp.where(logits == m1, lane, LANES), axis=-1, keepdims=True)
    rest = jnp.where(lane == i1, -jnp.inf, logits)
    m2 = jnp.max(rest, axis=-1, keepdims=True)
    i2 = jnp.min(jnp.where(rest == m2, lane, LANES), axis=-1, keepdims=True)
    e2 = jnp.exp(m2 - m1)
    g1 = 1.0 / (1.0 + e2)
    g2 = e2 * g1
    sel1 = lane == i1
    sel2 = lane == i2
    chosen = jnp.where(sel1 | sel2, 1.0, 0.0)
    incl = jnp.dot(tri_ref[...], chosen.astype(BF16), preferred_element_type=F32)
    rank = incl - chosen + carry_s[...]
    r1 = jnp.sum(jnp.where(sel1, rank, 0.0), axis=-1, keepdims=True)
    r2 = jnp.sum(jnp.where(sel2, rank, 0.0), axis=-1, keepdims=True)
    total = carry_s[...] + incl[ROW_BLOCK - 1:ROW_BLOCK, :]
    carry_s[...] = total
    counts_ref[...] = jnp.broadcast_to(total, counts_ref.shape)
    meta = jnp.zeros_like(logits)
    for ln, val in ((META_E1, i1.astype(F32)), (META_E2, i2.astype(F32)), (META_R1, r1), (META_R2, r2),
                    (META_G1, g1), (META_G2, g2)):
        meta = jnp.where(lane == ln, val, meta)
    meta_ref[...] = meta
    metat_ref[...] = meta.T[:SUBLANES, :]


def _outproj(st, h, yrec, yatt, w_out, g_ffn):
    R = ROW_BLOCK
    row = lambda i: (i, 0)
    const = lambda i: (0, 0)
    return pl.pallas_call(
        _outproj_kernel,
        grid=(st.rows // R,),
        in_specs=[pl.BlockSpec((R, D_MODEL), row), pl.BlockSpec((R, REC_WIDTH), row),
                  pl.BlockSpec((R, ATT_WIDTH), row), pl.BlockSpec((D_MODEL, D_MODEL), const),
                  pl.BlockSpec((1, D_MODEL), const)],
        out_specs=[pl.BlockSpec((R, D_MODEL), row), pl.BlockSpec((R, D_MODEL), row)],
        out_shape=[jax.ShapeDtypeStruct((st.rows, D_MODEL), F32), jax.ShapeDtypeStruct((st.rows, D_MODEL), BF16)],
        compiler_params=_params("parallel"), name="outproj",
    )(h, yrec, yatt, w_out, g_ffn)


def _outproj_router(st, h, yrec, yatt, w_out, g_ffn, w_router, b_router):
    R = ROW_BLOCK
    row = lambda i: (i, 0)
    const = lambda i: (0, 0)
    w_hi = w_router.astype(BF16)
    w_lo = (w_router - w_hi.astype(F32)).astype(BF16)
    wr = jnp.zeros((D_MODEL, LANES), BF16).at[:, :N_EXPERTS].set(w_hi).at[:, N_EXPERTS:2 * N_EXPERTS].set(w_lo)
    br = jnp.zeros((1, LANES), F32).at[0, :N_EXPERTS].set(b_router)
    tri = jnp.tril(jnp.ones((R, R), BF16))
    half = D_MODEL // 2
    return pl.pallas_call(
        _outproj_router_kernel,
        grid=(st.rows // R,),
        in_specs=[pl.BlockSpec((R, D_MODEL), row), pl.BlockSpec((R, REC_WIDTH), row),
                  pl.BlockSpec((R, ATT_WIDTH), row), pl.BlockSpec((D_MODEL, D_MODEL), const),
                  pl.BlockSpec((1, D_MODEL), const), pl.BlockSpec((D_MODEL, LANES), const),
                  pl.BlockSpec((1, LANES), const), pl.BlockSpec((R, R), const)],
        out_specs=[pl.BlockSpec((R, D_MODEL), row), pl.BlockSpec((R, half), row), pl.BlockSpec((R, LANES), row),
                   pl.BlockSpec((SUBLANES, R), lambda i: (0, i)), pl.BlockSpec((SUBLANES, LANES), const)],
        out_shape=[jax.ShapeDtypeStruct((st.rows, D_MODEL), F32), jax.ShapeDtypeStruct((st.rows, half), jnp.uint32),
                   jax.ShapeDtypeStruct((st.rows, LANES), F32), jax.ShapeDtypeStruct((SUBLANES, st.rows), F32),
                   jax.ShapeDtypeStruct((SUBLANES, LANES), F32)],
        scratch_shapes=[pltpu.VMEM((1, LANES), F32)],
        compiler_params=_params("arbitrary"), name="outproj_router",
    )(h, yrec, yatt, w_out, g_ffn, wr, br, tri)


def _ffn_kernel(x_ref, h_ref, w1_ref, w3_ref, w2_ref, out_ref, acc_s):
    j = pl.program_id(1)
    x = x_ref[...]
    a = jnp.dot(x, w1_ref[...], preferred_element_type=F32)
    b = jnp.dot(x, w3_ref[...], preferred_element_type=F32)
    hh = (a * jax.nn.sigmoid(a) * b).astype(BF16)
    y = jnp.dot(hh, w2_ref[...], preferred_element_type=F32)

    @pl.when(j == 0)
    def _():
        acc_s[...] = h_ref[...] + y

    @pl.when(j > 0)
    def _():
        acc_s[...] += y

    @pl.when(j == pl.num_programs(1) - 1)
    def _():
        out_ref[...] = acc_s[...]


def _ffn_dense(st, hn, h, w1, w3, w2):
    RB, FC = FFN_ROWS, FF_CHUNK
    row = lambda i, j: (i, 0)
    return pl.pallas_call(
        _ffn_kernel,
        grid=(st.rows // RB, D_FF // FC),
        in_specs=[pl.BlockSpec((RB, D_MODEL), row), pl.BlockSpec((RB, D_MODEL), row),
                  pl.BlockSpec((D_MODEL, FC), lambda i, j: (0, j)),
                  pl.BlockSpec((D_MODEL, FC), lambda i, j: (0, j)),
                  pl.BlockSpec((FC, D_MODEL), lambda i, j: (j, 0))],
        out_specs=pl.BlockSpec((RB, D_MODEL), row),
        out_shape=jax.ShapeDtypeStruct((st.rows, D_MODEL), F32),
        scratch_shapes=[pltpu.VMEM((RB, D_MODEL), F32)],
        compiler_params=_params("parallel", "arbitrary"),
        name="ffn_dense",
    )(hn, h, w1, w3, w2)


def _row_copy(src_ref, src_row, dst_ref, dst_row, sem):
    return pltpu.make_async_copy(src_ref.at[pl.ds(src_row, 1)], dst_ref.at[pl.ds(dst_row, 1)], sem)


def _dispatch_kernel(pos1_ref, pos2_ref, src_ref, init_ref, dst_ref, sem):
    del init_ref
    base = pl.program_id(0) * MOVE_ROWS

    def copies(t):
        return (_row_copy(src_ref, base + t, dst_ref, pos1_ref[t], sem),
                _row_copy(src_ref, base + t, dst_ref, pos2_ref[t], sem))

    def start(t, _):
        for c in copies(t):
            c.start()
        return 0

    def wait(t, _):
        for c in copies(t):
            c.wait()
        return 0

    lax.fori_loop(0, MOVE_ROWS, start, 0, unroll=8)
    lax.fori_loop(0, MOVE_ROWS, wait, 0, unroll=8)


def _dispatch(st, hn_packed, pos1, pos2, n_slots):
    half = D_MODEL // 2
    idx = pl.BlockSpec((MOVE_ROWS,), lambda i: (i,), memory_space=pltpu.SMEM)
    anywhere = pl.BlockSpec(memory_space=pl.ANY)
    return pl.pallas_call(
        _dispatch_kernel,
        grid=(st.rows // MOVE_ROWS,),
        in_specs=[idx, idx, anywhere, anywhere],
        out_specs=anywhere,
        out_shape=jax.ShapeDtypeStruct((n_slots, half), jnp.uint32),
        scratch_shapes=[pltpu.SemaphoreType.DMA],
        input_output_aliases={3: 0},
        compiler_params=_params("arbitrary"),
        name="moe_dispatch",
    )(pos1, pos2, hn_packed, jnp.zeros((n_slots, half), jnp.uint32))


def _moe_ffn_kernel(expert_ref, used_ref, x_ref, w1_ref, w3_ref, w2_ref, y_ref, xb_s, acc_s):
    del expert_ref
    half = D_MODEL // 2
    j = pl.program_id(1)

    @pl.when(pl.program_id(0) < used_ref[0])
    def _():
        @pl.when(j == 0)
        def _():
            lo, hi = _unpack_halves(x_ref[...])
            xb_s[:, :half] = lo.astype(BF16)
            xb_s[:, half:] = hi.astype(BF16)

        x = xb_s[...]
        a = jnp.dot(x, w1_ref[...], preferred_element_type=F32)
        b = jnp.dot(x, w3_ref[...], preferred_element_type=F32)
        hh = (a * jax.nn.sigmoid(a) * b).astype(BF16)
        y = jnp.dot(hh, w2_ref[...], preferred_element_type=F32)

        @pl.when(j == 0)
        def _():
            acc_s[...] = y

        @pl.when(j > 0)
        def _():
            acc_s[...] += y

        @pl.when(j == pl.num_programs(1) - 1)
        def _():
            y_ref[...] = _pack_halves(acc_s[...])


def _moe_ffn(x_sorted, block_expert, n_used, w1, w3, w2):
    RB, FC = FFN_ROWS, FF_CHUNK
    half = D_MODEL // 2
    nj = D_FF // FC
    n_blocks = x_sorted.shape[0] // RB

    def blk(i, used):
        return jnp.minimum(i, used[0] - 1)

    def chunk(i, j, used):
        return jnp.where(i < used[0], j, nj - 1)

    row = lambda i, j, ex, used: (blk(i, used), 0)
    return pl.pallas_call(
        _moe_ffn_kernel,
        grid_spec=pltpu.PrefetchScalarGridSpec(
            num_scalar_prefetch=2,
            grid=(n_blocks, nj),
            in_specs=[pl.BlockSpec((RB, half), row),
                      pl.BlockSpec((None, D_MODEL, FC), lambda i, j, ex, used: (ex[blk(i, used)], 0, chunk(i, j, used))),
                      pl.BlockSpec((None, D_MODEL, FC), lambda i, j, ex, used: (ex[blk(i, used)], 0, chunk(i, j, used))),
                      pl.BlockSpec((None, FC, D_MODEL), lambda i, j, ex, used: (ex[blk(i, used)], chunk(i, j, used), 0))],
            out_specs=pl.BlockSpec((RB, half), row),
            scratch_shapes=[pltpu.VMEM((RB, D_MODEL), BF16), pltpu.VMEM((RB, D_MODEL), F32)],
        ),
        out_shape=jax.ShapeDtypeStruct(x_sorted.shape, jnp.uint32),
        compiler_params=_params("arbitrary", "arbitrary"),
        name="moe_ffn",
    )(block_expert, n_used, x_sorted, w1, w3, w2)


def _combine_kernel(pos1_ref, pos2_ref, h_ref, meta_ref, g_ref, y_ref, out_ref, y1_s, y2_s, sem):
    half = D_MODEL // 2

    def copies(t):
        return (_row_copy(y_ref, pos1_ref[t], y1_s, t, sem), _row_copy(y_ref, pos2_ref[t], y2_s, t, sem))

    def start(t, _):
        for c in copies(t):
            c.start()
        return 0

    def wait(t, _):
        for c in copies(t):
            c.wait()
        return 0

    lax.fori_loop(0, MOVE_ROWS, start, 0, unroll=8)
    lax.fori_loop(0, MOVE_ROWS, wait, 0, unroll=8)
    meta = meta_ref[...]
    g1 = meta[:, META_G1:META_G1 + 1]
    g2 = meta[:, META_G2:META_G2 + 1]
    lo1, hi1 = _unpack_halves(y1_s[...])
    lo2, hi2 = _unpack_halves(y2_s[...])
    lo = h_ref[:, :half] + g1 * lo1 + g2 * lo2
    hi = h_ref[:, half:] + g1 * hi1 + g2 * hi2
    ms = (jnp.sum(lo * lo, axis=-1, keepdims=True) + jnp.sum(hi * hi, axis=-1, keepdims=True)) * (1.0 / D_MODEL)
    inv = lax.rsqrt(ms + EPS)
    out_ref[:, :half] = lo * inv * g_ref[:, :half]
    out_ref[:, half:] = hi * inv * g_ref[:, half:]


def _combine(y_sorted, h, meta, pos1, pos2, g_final, row0, n_rows):
    half = D_MODEL // 2
    b0 = row0 // MOVE_ROWS
    idx = pl.BlockSpec((MOVE_ROWS,), lambda i: (b0 + i,), memory_space=pltpu.SMEM)
    row = lambda i: (b0 + i, 0)
    return pl.pallas_call(
        _combine_kernel,
        grid=(n_rows // MOVE_ROWS,),
        in_specs=[idx, idx, pl.BlockSpec((MOVE_ROWS, D_MODEL), row), pl.BlockSpec((MOVE_ROWS, LANES), row),
                  pl.BlockSpec((1, D_MODEL), lambda i: (0, 0)), pl.BlockSpec(memory_space=pl.ANY)],
        out_specs=pl.BlockSpec((MOVE_ROWS, D_MODEL), lambda i: (i, 0)),
        out_shape=jax.ShapeDtypeStruct((n_rows, D_MODEL), F32),
        scratch_shapes=[pltpu.VMEM((MOVE_ROWS, half), jnp.uint32), pltpu.VMEM((MOVE_ROWS, half), jnp.uint32),
                        pltpu.SemaphoreType.DMA],
        compiler_params=_params("arbitrary"),
        name="moe_combine",
    )(pos1, pos2, h, meta, g_final, y_sorted)


def _routing_tables(metat, counts, n_blocks):
    RB = FFN_ROWS
    cnt = counts[0, :N_EXPERTS].astype(jnp.int32)
    padded = (cnt + RB - 1) // RB * RB
    ends = jnp.cumsum(padded)
    starts = ends - padded
    e1 = metat[META_E1].astype(jnp.int32)
    e2 = metat[META_E2].astype(jnp.int32)
    pos1 = starts[e1] + metat[META_R1].astype(jnp.int32)
    pos2 = starts[e2] + metat[META_R2].astype(jnp.int32)
    block_start = jnp.arange(n_blocks, dtype=jnp.int32) * RB
    block_expert = jnp.minimum(jnp.searchsorted(ends, block_start, side="right"), N_EXPERTS - 1).astype(jnp.int32)
    n_used = (ends[-1:] // RB).astype(jnp.int32)
    return pos1, pos2, block_expert, n_used


def _block_diag(w):
    nb, bs, _ = w.shape
    eye = jnp.eye(nb, dtype=w.dtype)
    return jnp.einsum("ncf,nm->ncmf", w, eye).reshape(nb * bs, nb * bs)


def kernel(x_prompt, x_sample, norm_mix, w_in, conv_w, conv_b, w_rgate, b_rgate, w_igate, b_igate,
           lru_lambda, attn_sink, norm_rec_out, norm_attn_out, w_out, norm_ffn, ffn_w1, ffn_w3, ffn_w2,
           moe_router, moe_router_bias, moe_w1, moe_w3, moe_w2, norm_final):
    n_long, s_long, _ = x_prompt.shape
    n_short, s_short, _ = x_sample.shape
    depth = w_in.shape[0]
    assert depth == 2, "layer 0 uses the dense feed-forward, layer 1 the mixture of experts"
    for s in (s_long, s_short):
        assert s % max(ROW_BLOCK, ATT_BLOCK) == 0
    st = Stream(n_long, s_long, n_short, s_short)
    assert st.rows % FFN_ROWS == 0
    assert st.n_long_rows % MOVE_ROWS == 0 and st.rows % MOVE_ROWS == 0

    h = jnp.concatenate([x_prompt.reshape(-1, D_MODEL), x_sample.reshape(-1, D_MODEL)], axis=0)
    tables = _rope_tables(max(s_long, s_short))
    row2 = lambda a: a.reshape(1, -1)

    for l in range(depth):
        xrec, gate, q, k, v = _inproj(st, h, row2(norm_mix[l]), w_in[l].astype(BF16), tables)
        wg = [jnp.concatenate([_block_diag(w_rgate[l, d]), _block_diag(w_igate[l, d])], axis=1).astype(BF16)
              for d in range(2)]
        bg = [row2(jnp.concatenate([b_rgate[l, d], b_igate[l, d]])) for d in range(2)]
        yrec = _recurrent(st, xrec, gate, conv_w[l], row2(conv_b[l]), wg[0], bg[0], wg[1], bg[1],
                          row2(lru_lambda[l, 0]), row2(lru_lambda[l, 1]), row2(norm_rec_out[l]))
        yatt = _attention(st, q, k, v, attn_sink[l], row2(norm_attn_out[l]))
        if l % 2 == 0:
            j = l // 2
            h, hn = _outproj(st, h, yrec, yatt, w_out[l].astype(BF16), row2(norm_ffn[l]))
            h = _ffn_dense(st, hn, h, ffn_w1[j].astype(BF16), ffn_w3[j].astype(BF16), ffn_w2[j].astype(BF16))
        else:
            j = l // 2
            h, hn_packed, meta, metat, counts = _outproj_router(
                st, h, yrec, yatt, w_out[l].astype(BF16), row2(norm_ffn[l]), moe_router[j], moe_router_bias[j])
            n_blocks = TOP_K * st.rows // FFN_ROWS + N_EXPERTS
            pos1, pos2, block_expert, n_used = _routing_tables(metat, counts, n_blocks)
            x_sorted = _dispatch(st, hn_packed, pos1, pos2, n_blocks * FFN_ROWS)
            y_sorted = _moe_ffn(x_sorted, block_expert, n_used, moe_w1[j].astype(BF16), moe_w3[j].astype(BF16),
                                moe_w2[j].astype(BF16))
            outs = [_combine(y_sorted, h, meta, pos1, pos2, row2(norm_final), r0, n)
                    for r0, n in ((0, st.n_long_rows), (st.n_long_rows, st.rows - st.n_long_rows))]
    return (outs[0].reshape(x_prompt.shape), outs[1].reshape(x_sample.shape))
```

```python
import functools

import jax
import jax.numpy as jnp
from jax import lax
from jax.experimental import pallas as pl
from jax.experimental.pallas import tpu as pltpu

F32 = jnp.float32
BF16 = jnp.bfloat16

D_MODEL = 1024
REC_WIDTH = 512
N_REC_BLOCKS = 8
REC_BLOCK = 64
CONV_WIDTH = 4
LRU_C = 8.0
HEAD_DIM = 64
N_Q_HEADS = 8
N_KV_HEADS = 2
Q_PER_KV = 4
ATT_WIDTH = 512
KV_WIDTH = 128
ROT_DIM = 16
ROPE_THETA = 500000.0
WINDOW = 128
D_FF = 3584
N_EXPERTS = 8
EPS = 1e-6
IN_COLS = 2 * REC_WIDTH + ATT_WIDTH + 2 * KV_WIDTH
NEG_INF = -1e30

LANES = 128
SUBLANES = 8
HALO_ROWS = 2 * SUBLANES
VMEM_LIMIT = 56 * 1024 * 1024

ROW_BLOCK = 512
SCAN_STEPS = ROW_BLOCK // SUBLANES
ATT_BLOCK = 512
FFN_ROWS = 1024
FF_CHUNK = 512
FFN_SUB_ROWS = 512
MOVE_ROWS = 1024
TOP_K = 2


class Stream:
    def __init__(self, n_long, s_long, n_short, s_short):
        self.n_long_rows = n_long * s_long
        self.s_long = s_long
        self.s_short = s_short
        self.rows = n_long * s_long + n_short * s_short

    def seq_pos(self, row0):
        in_long = row0 < self.n_long_rows
        pos = jnp.where(in_long, lax.rem(row0, self.s_long),
                        lax.rem(jnp.maximum(row0 - self.n_long_rows, 0), self.s_short))
        slen = jnp.where(in_long, self.s_long, self.s_short)
        return pos, slen


def _params(*sem):
    return pltpu.CompilerParams(dimension_semantics=sem, vmem_limit_bytes=VMEM_LIMIT)


def _rms(x, g):
    return x * lax.rsqrt(jnp.mean(x * x, axis=-1, keepdims=True) + EPS) * g


def _interleave_matrix():
    p = jnp.arange(ROW_BLOCK)
    src = (p % SUBLANES) * SCAN_STEPS + p // SUBLANES
    return (src[:, None] == jnp.arange(ROW_BLOCK)[None, :]).astype(BF16)


def _inproj_kernel(x_ref, g_ref, w_ref, perm_ref, c_ref, sa_ref, sb_ref,
                   xrec_ref, gate_ref, q_ref, k_ref, v_ref):
    xn = _rms(x_ref[...], g_ref[...]).astype(BF16)
    proj = jnp.dot(xn, w_ref[...], preferred_element_type=F32)
    rec = jnp.dot(perm_ref[...], proj[:, :2 * REC_WIDTH].astype(BF16), preferred_element_type=F32)
    xrec_ref[...] = rec[:, :REC_WIDTH].astype(BF16)
    gate_ref[...] = rec[:, REC_WIDTH:].astype(BF16)
    c, sa, sb = c_ref[...], sa_ref[...], sb_ref[...]

    def rope(t):
        return (t * c + pltpu.roll(t, LANES - ROT_DIM // 2, 1) * sa
                + pltpu.roll(t, ROT_DIM // 2, 1) * sb)

    q0 = 2 * REC_WIDTH
    scale = HEAD_DIM ** -0.5
    for j in range(ATT_WIDTH // LANES):
        t = proj[:, q0 + j * LANES:q0 + (j + 1) * LANES]
        q_ref[:, j * LANES:(j + 1) * LANES] = (rope(t) * scale).astype(BF16)
    k0 = q0 + ATT_WIDTH
    k_ref[...] = rope(proj[:, k0:k0 + KV_WIDTH]).astype(BF16)
    v_ref[...] = proj[:, k0 + KV_WIDTH:k0 + 2 * KV_WIDTH].astype(BF16)


def _rope_tables(s_max):
    inv_freq = ROPE_THETA ** (-jnp.arange(0, ROT_DIM, 2, dtype=F32) / ROT_DIM)
    ang = jnp.arange(s_max, dtype=F32)[:, None] * inv_freq[None, :]
    cos, sin = jnp.cos(ang), jnp.sin(ang)
    half = ROT_DIM // 2
    ones = jnp.ones((s_max, HEAD_DIM - ROT_DIM), F32)
    zeros_h = jnp.zeros((s_max, half), F32)
    zeros_r = jnp.zeros((s_max, HEAD_DIM - ROT_DIM), F32)
    c = jnp.concatenate([cos, cos, ones], axis=1)
    sa = jnp.concatenate([-sin, zeros_h, zeros_r], axis=1)
    sb = jnp.concatenate([zeros_h, sin, zeros_r], axis=1)
    reps = LANES // HEAD_DIM
    return tuple(jnp.tile(t, (1, reps)) for t in (c, sa, sb))


def _inproj(st, h, g, w_in, tables):
    R = ROW_BLOCK
    nblk = st.rows // R

    def pos_blk(i):
        pos, _ = st.seq_pos(i * R)
        return pos // R

    row = lambda i: (i, 0)
    const = lambda i: (0, 0)
    tab = pl.BlockSpec((R, LANES), lambda i: (pos_blk(i), 0))
    outs = [(REC_WIDTH, BF16), (REC_WIDTH, BF16), (ATT_WIDTH, BF16), (KV_WIDTH, BF16), (KV_WIDTH, BF16)]
    return pl.pallas_call(
        _inproj_kernel,
        grid=(nblk,),
        in_specs=[pl.BlockSpec((R, D_MODEL), row), pl.BlockSpec((1, D_MODEL), const),
                  pl.BlockSpec((D_MODEL, IN_COLS), const), pl.BlockSpec((R, R), const), tab, tab, tab],
        out_specs=[pl.BlockSpec((R, w), row) for w, _ in outs],
        out_shape=[jax.ShapeDtypeStruct((st.rows, w), dt) for w, dt in outs],
        compiler_params=_params("parallel"),
        name="inproj",
    )(h, g, w_in, _interleave_matrix(), *tables)


def _softplus(x):
    return jnp.maximum(x, 0.0) + jnp.log1p(jnp.exp(-jnp.abs(x)))


def _conv_gates(st, blk, x_ref, xp_ref, xn_ref, cw_ref, cb_ref, wg_ref, bg_ref, lam_ref, a_s, u_s):
    R, S = ROW_BLOCK, SUBLANES
    pos, slen = st.seq_pos(blk * R)
    has_prev = (pos > 0).astype(F32)
    has_next = (pos + R < slen).astype(F32)
    x = x_ref[...].astype(F32)
    sub = lax.broadcasted_iota(jnp.int32, (S, REC_WIDTH), 0)
    prev_rows = xp_ref[...].astype(F32) * has_prev
    next_rows = xn_ref[...].astype(F32) * has_next
    wrap_m1 = jnp.where(sub == 0, prev_rows[2 * S - 1:2 * S], pltpu.roll(x[R - S:], 1, 0))
    wrap_m2 = jnp.where(sub == 0, prev_rows[S - 1:S], pltpu.roll(x[R - 2 * S:R - S], 1, 0))
    wrap_p1 = jnp.where(sub == S - 1, next_rows[0:1], pltpu.roll(x[:S], S - 1, 0))
    taps = (jnp.concatenate([wrap_m2, wrap_m1, x[:R - 2 * S]], axis=0),
            jnp.concatenate([wrap_m1, x[:R - S]], axis=0),
            x,
            jnp.concatenate([x[S:], wrap_p1], axis=0))
    assert len(taps) == CONV_WIDTH and CONV_WIDTH // 2 == 2
    xc = cb_ref[...]
    for t in range(CONV_WIDTH):
        xc = xc + taps[t] * cw_ref[t:t + 1, :]
    pre = jnp.dot(xc.astype(BF16), wg_ref[...], preferred_element_type=F32) + bg_ref[...]
    r = 0.5 * jnp.tanh(0.5 * pre[:, :REC_WIDTH]) + 0.5
    i = 0.5 * jnp.tanh(0.5 * pre[:, REC_WIDTH:]) + 0.5
    a = jnp.exp(r * ((-LRU_C) * _softplus(-lam_ref[...])))
    a_s[...] = a
    u_s[...] = jnp.sqrt(1.0 - a * a) * (i * xc)


def _scan_block(a_s, u_s, out_ref, carry_in, reverse):
    S = SUBLANES
    width = a_s.shape[1]

    def group(j):
        jj = SCAN_STEPS - 1 - j if reverse else j
        return pl.ds(pl.multiple_of(jj * S, S), S)

    def local_step(j, hp):
        h, p = hp
        a = a_s[group(j), :]
        return a * h + u_s[group(j), :], a * p

    h_end, p_end = lax.fori_loop(0, SCAN_STEPS, local_step,
                                 (jnp.zeros((S, width), F32), jnp.ones((S, width), F32)), unroll=8)
    enter = [None] * S
    cur = carry_in
    for s in (range(S - 1, -1, -1) if reverse else range(S)):
        enter[s] = cur
        cur = h_end[s:s + 1] + p_end[s:s + 1] * cur

    def true_step(j, h):
        h = a_s[group(j), :] * h + u_s[group(j), :]
        out_ref[group(j), :] = h
        return h

    lax.fori_loop(0, SCAN_STEPS, true_step, jnp.concatenate(enter, axis=0), unroll=8)
    return cur


def _rec_fwd_kernel(st, x_ref, xp_ref, xn_ref, cw_ref, cb_ref, wg_ref, bg_ref, lam_ref,
                    hf_ref, a_s, u_s, carry_s):
    blk = pl.program_id(0)
    _conv_gates(st, blk, x_ref, xp_ref, xn_ref, cw_ref, cb_ref, wg_ref, bg_ref, lam_ref, a_s, u_s)
    pos, _ = st.seq_pos(blk * ROW_BLOCK)
    carry_in = jnp.where(pos > 0, carry_s[...], 0.0)
    carry_s[...] = _scan_block(a_s, u_s, hf_ref, carry_in, reverse=False)


def _rec_bwd_kernel(st, nblk, x_ref, xp_ref, xn_ref, cw_ref, cb_ref, wg_ref, bg_ref, lam_ref,
                    hf_ref, gate_ref, gout_ref, unperm_ref, y_ref, a_s, u_s, hb_s, carry_s):
    blk = nblk - 1 - pl.program_id(0)
    _conv_gates(st, blk, x_ref, xp_ref, xn_ref, cw_ref, cb_ref, wg_ref, bg_ref, lam_ref, a_s, u_s)
    pos, slen = st.seq_pos(blk * ROW_BLOCK)
    carry_in = jnp.where(pos + ROW_BLOCK < slen, carry_s[...], 0.0)
    carry_s[...] = _scan_block(a_s, u_s, hb_s, carry_in, reverse=True)
    y = (hf_ref[...] + hb_s[...]) * jax.nn.gelu(gate_ref[...].astype(F32), approximate=True)
    y = _rms(y, gout_ref[...]).astype(BF16)
    y_ref[...] = jnp.dot(unperm_ref[...], y, preferred_element_type=F32).astype(BF16)


def _recurrent(st, xrec, gate, conv_w, conv_b, wg_f, bg_f, wg_b, bg_b, lam_f, lam_b, g_rec_out):
    R = ROW_BLOCK
    nblk = st.rows // R
    hpb = R // HALO_ROWS
    n_halo = st.rows // HALO_ROWS
    const = lambda i: (0, 0)

    def specs(order):
        cur = lambda i: (order(i), 0)
        prev = lambda i: (jnp.maximum(order(i) * hpb - 1, 0), 0)
        nxt = lambda i: (jnp.minimum((order(i) + 1) * hpb, n_halo - 1), 0)
        return [pl.BlockSpec((R, REC_WIDTH), cur), pl.BlockSpec((HALO_ROWS, REC_WIDTH), prev),
                pl.BlockSpec((HALO_ROWS, REC_WIDTH), nxt),
                pl.BlockSpec((CONV_WIDTH, REC_WIDTH), const), pl.BlockSpec((1, REC_WIDTH), const),
                pl.BlockSpec((REC_WIDTH, 2 * REC_WIDTH), const), pl.BlockSpec((1, 2 * REC_WIDTH), const),
                pl.BlockSpec((1, REC_WIDTH), const)], cur

    block = (R, REC_WIDTH)
    work = [pltpu.VMEM(block, F32) for _ in range(2)]
    carry = pltpu.VMEM((1, REC_WIDTH), F32)
    in_f, cur_f = specs(lambda i: i)
    hf = pl.pallas_call(
        functools.partial(_rec_fwd_kernel, st),
        grid=(nblk,),
        in_specs=in_f,
        out_specs=pl.BlockSpec(block, cur_f),
        out_shape=jax.ShapeDtypeStruct((st.rows, REC_WIDTH), F32),
        scratch_shapes=work + [carry],
        compiler_params=_params("arbitrary"),
        name="rec_fwd",
    )(xrec, xrec, xrec, conv_w, conv_b, wg_f, bg_f, lam_f)
    in_b, cur_b = specs(lambda i: nblk - 1 - i)
    return pl.pallas_call(
        functools.partial(_rec_bwd_kernel, st, nblk),
        grid=(nblk,),
        in_specs=in_b + [pl.BlockSpec(block, cur_b), pl.BlockSpec(block, cur_b),
                         pl.BlockSpec((1, REC_WIDTH), const), pl.BlockSpec((R, R), const)],
        out_specs=pl.BlockSpec(block, cur_b),
        out_shape=jax.ShapeDtypeStruct((st.rows, REC_WIDTH), BF16),
        scratch_shapes=work + [pltpu.VMEM(block, F32), carry],
        compiler_params=_params("arbitrary"),
        name="rec_bwd",
    )(xrec, xrec, xrec, conv_w, conv_b, wg_b, bg_b, lam_b, hf, gate, g_rec_out, _interleave_matrix().T)


def _attn_kernel(st, sink_ref, q_ref, k_ref, kp_ref, kn_ref, v_ref, vp_ref, vn_ref, g_ref,
                 y_ref, acc_s):
    QB = ATT_BLOCK
    W = WINDOW
    assert LANES == 2 * HEAD_DIM and KV_WIDTH == LANES and N_KV_HEADS == 2 and Q_PER_KV == 4
    pos0, slen = st.seq_pos(pl.program_id(0) * QB)
    kx = jnp.concatenate([kp_ref[...], k_ref[...], kn_ref[...]], axis=0)
    kx_swapped = jnp.concatenate([kx[:, HEAD_DIM:], kx[:, :HEAD_DIM]], axis=1)
    lane = lax.broadcasted_iota(jnp.int32, kx.shape, 1)
    zero = jnp.zeros_like(kx)
    low, high = lane < HEAD_DIM, lane >= HEAD_DIM
    k_sel = ((jnp.where(low, kx, zero), jnp.where(high, kx_swapped, zero)),
             (jnp.where(low, kx_swapped, zero), jnp.where(high, kx, zero)))
    vx = jnp.concatenate([vp_ref[...], v_ref[...], vn_ref[...]], axis=0).astype(F32)
    vt = vx.T.astype(BF16)
    kj = lax.broadcasted_iota(jnp.int32, (3 * W, 2 * W), 0)
    qi = lax.broadcasted_iota(jnp.int32, (3 * W, 2 * W), 1) % W
    band = jnp.abs(kj - W - qi) <= W
    for b in range(QB // W):
        kpos = pos0 + (b - 1) * W + kj
        bias = jnp.where(band & (kpos >= 0) & (kpos < slen), 0.0, NEG_INF)
        for g in range(N_KV_HEADS):
            qg = jnp.concatenate([q_ref[b * W:(b + 1) * W, (2 * g + c) * LANES:(2 * g + c + 1) * LANES]
                                  for c in range(2)], axis=0)
            vg = vt[g * HEAD_DIM:(g + 1) * HEAD_DIM, b * W:(b + 3) * W]
            outs = []
            for parity in range(2):
                kg = k_sel[g][parity][b * W:(b + 3) * W]
                s = lax.dot_general(kg, qg, (((1,), (1,)), ((), ())), preferred_element_type=F32) + bias
                sink = jnp.concatenate([jnp.full((1, W), sink_ref[4 * g + 2 * c + parity], F32)
                                        for c in range(2)], axis=1)
                m = jnp.maximum(jnp.max(s, axis=0, keepdims=True), sink)
                p = jnp.exp(s - m)
                denom = jnp.sum(p, axis=0, keepdims=True) + jnp.exp(sink - m)
                o = jnp.dot(vg, p.astype(BF16), preferred_element_type=F32)
                outs.append(o / denom)
            o_t = jnp.concatenate(outs, axis=0).T
            for c in range(2):
                acc_s[b * W:(b + 1) * W, (2 * g + c) * LANES:(2 * g + c + 1) * LANES] = o_t[c * W:(c + 1) * W]
    y_ref[...] = _rms(acc_s[...], g_ref[...]).astype(BF16)


def _attention(st, q, k, v, sink, g_attn_out):
    QB = ATT_BLOCK
    W = WINDOW
    nblk = st.rows // QB
    wpb = QB // W
    n_w = st.rows // W
    cur = lambda i, s: (i, 0)
    prev = lambda i, s: (jnp.maximum(i * wpb - 1, 0), 0)
    nxt = lambda i, s: (jnp.minimum((i + 1) * wpb, n_w - 1), 0)
    kv = [pl.BlockSpec((QB, KV_WIDTH), cur), pl.BlockSpec((W, KV_WIDTH), prev), pl.BlockSpec((W, KV_WIDTH), nxt)]
    return pl.pallas_call(
        functools.partial(_attn_kernel, st),
        grid_spec=pltpu.PrefetchScalarGridSpec(
            num_scalar_prefetch=1,
            grid=(nblk,),
            in_specs=[pl.BlockSpec((QB, ATT_WIDTH), cur)] + kv + kv
                     + [pl.BlockSpec((1, ATT_WIDTH), lambda i, s: (0, 0))],
            out_specs=pl.BlockSpec((QB, ATT_WIDTH), cur),
            scratch_shapes=[pltpu.VMEM((QB, ATT_WIDTH), F32)],
        ),
        out_shape=jax.ShapeDtypeStruct((st.rows, ATT_WIDTH), BF16),
        compiler_params=_params("parallel"),
        name="attention",
    )(sink, q, k, k, k, v, v, v, g_attn_out)


def _outproj_kernel(h_ref, yr_ref, ya_ref, w_ref, g_ref, hout_ref, hn_ref):
    y = (jnp.dot(yr_ref[...], w_ref[:REC_WIDTH, :], preferred_element_type=F32)
         + jnp.dot(ya_ref[...], w_ref[REC_WIDTH:, :], preferred_element_type=F32))
    h = h_ref[...] + y
    hout_ref[...] = h
    hn_ref[...] = _rms(h, g_ref[...]).astype(BF16)


def _pack_halves(x):
    n = x.shape[1] // 2

    def bf16_bits(v):
        b = lax.bitcast_convert_type(v, jnp.uint32)
        return (b + jnp.uint32(0x7FFF) + ((b >> 16) & jnp.uint32(1))) >> 16

    return bf16_bits(x[:, :n]) | (bf16_bits(x[:, n:]) << 16)


def _unpack_halves(w):
    lo = lax.bitcast_convert_type(w << 16, F32)
    hi = lax.bitcast_convert_type(w & jnp.uint32(0xFFFF0000), F32)
    return lo, hi


META_E1, META_E2, META_R1, META_R2, META_G1, META_G2 = range(6)


def _outproj_router_kernel(h_ref, yr_ref, ya_ref, w_ref, g_ref, wr_ref, br_ref, tri_ref,
                           hout_ref, hnp_ref, meta_ref, metat_ref, counts_ref, carry_s):
    @pl.when(pl.program_id(0) == 0)
    def _():
        carry_s[...] = jnp.zeros_like(carry_s)

    y = (jnp.dot(yr_ref[...], w_ref[:REC_WIDTH, :], preferred_element_type=F32)
         + jnp.dot(ya_ref[...], w_ref[REC_WIDTH:, :], preferred_element_type=F32))
    h = h_ref[...] + y
    hout_ref[...] = h
    hn = _rms(h, g_ref[...])
    hnp_ref[...] = _pack_halves(hn)
    hi = hn.astype(BF16)
    lo = (hn - hi.astype(F32)).astype(BF16)
    a = jnp.dot(hi, wr_ref[...], preferred_element_type=F32)
    b = jnp.dot(lo, wr_ref[...], preferred_element_type=F32)
    logits = a + pltpu.roll(a, LANES - N_EXPERTS, 1) + b + br_ref[...]
    lane = lax.broadcasted_iota(jnp.int32, logits.shape, 1)
    logits = jnp.where(lane < N_EXPERTS, logits, -jnp.inf)
    m1 = jnp.max(logits, axis=-1, keepdims=True)
    i1 = jnp.min(jnp.where(logits == m1, lane, LANES), axis=-1, keepdims=True)
    rest = jnp.where(lane == i1, -jnp.inf, logits)
    m2 = jnp.max(rest, axis=-1, keepdims=True)
    i2 = jnp.min(jnp.where(rest == m2, lane, LANES), axis=-1, keepdims=True)
    e2 = jnp.exp(m2 - m1)
    g1 = 1.0 / (1.0 + e2)
    g2 = e2 * g1
    sel1 = lane == i1
    sel2 = lane == i2
    chosen = jnp.where(sel1 | sel2, 1.0, 0.0)
    incl = jnp.dot(tri_ref[...], chosen.astype(BF16), preferred_element_type=F32)
    rank = incl - chosen + carry_s[...]
    r1 = jnp.sum(jnp.where(sel1, rank, 0.0), axis=-1, keepdims=True)
    r2 = jnp.sum(jnp.where(sel2, rank, 0.0), axis=-1, keepdims=True)
    total = carry_s[...] + incl[ROW_BLOCK - 1:ROW_BLOCK, :]
    carry_s[...] = total
    counts_ref[...] = jnp.broadcast_to(total, counts_ref.shape)
    meta = jnp.zeros_like(logits)
    for ln, val in ((META_E1, i1.astype(F32)), (META_E2, i2.astype(F32)), (META_R1, r1), (META_R2, r2),
                    (META_G1, g1), (META_G2, g2)):
        meta = jnp.where(lane == ln, val, meta)
    meta_ref[...] = meta
    metat_ref[...] = meta.T[:SUBLANES, :]


def _outproj(st, h, yrec, yatt, w_out, g_ffn):
    R = ROW_BLOCK
    row = lambda i: (i, 0)
    const = lambda i: (0, 0)
    return pl.pallas_call(
        _outproj_kernel,
        grid=(st.rows // R,),
        in_specs=[pl.BlockSpec((R, D_MODEL), row), pl.BlockSpec((R, REC_WIDTH), row),
                  pl.BlockSpec((R, ATT_WIDTH), row), pl.BlockSpec((D_MODEL, D_MODEL), const),
                  pl.BlockSpec((1, D_MODEL), const)],
        out_specs=[pl.BlockSpec((R, D_MODEL), row), pl.BlockSpec((R, D_MODEL), row)],
        out_shape=[jax.ShapeDtypeStruct((st.rows, D_MODEL), F32), jax.ShapeDtypeStruct((st.rows, D_MODEL), BF16)],
        compiler_params=_params("parallel"), name="outproj",
    )(h, yrec, yatt, w_out, g_ffn)


def _outproj_router(st, h, yrec, yatt, w_out, g_ffn, w_router, b_router):
    R = ROW_BLOCK
    row = lambda i: (i, 0)
    const = lambda i: (0, 0)
    w_hi = w_router.astype(BF16)
    w_lo = (w_router - w_hi.astype(F32)).astype(BF16)
    wr = jnp.zeros((D_MODEL, LANES), BF16).at[:, :N_EXPERTS].set(w_hi).at[:, N_EXPERTS:2 * N_EXPERTS].set(w_lo)
    br = jnp.zeros((1, LANES), F32).at[0, :N_EXPERTS].set(b_router)
    tri = jnp.tril(jnp.ones((R, R), BF16))
    half = D_MODEL // 2
    return pl.pallas_call(
        _outproj_router_kernel,
        grid=(st.rows // R,),
        in_specs=[pl.BlockSpec((R, D_MODEL), row), pl.BlockSpec((R, REC_WIDTH), row),
                  pl.BlockSpec((R, ATT_WIDTH), row), pl.BlockSpec((D_MODEL, D_MODEL), const),
                  pl.BlockSpec((1, D_MODEL), const), pl.BlockSpec((D_MODEL, LANES), const),
                  pl.BlockSpec((1, LANES), const), pl.BlockSpec((R, R), const)],
        out_specs=[pl.BlockSpec((R, D_MODEL), row), pl.BlockSpec((R, half), row), pl.BlockSpec((R, LANES), row),
                   pl.BlockSpec((SUBLANES, R), lambda i: (0, i)), pl.BlockSpec((SUBLANES, LANES), const)],
        out_shape=[jax.ShapeDtypeStruct((st.rows, D_MODEL), F32), jax.ShapeDtypeStruct((st.rows, half), jnp.uint32),
                   jax.ShapeDtypeStruct((st.rows, LANES), F32), jax.ShapeDtypeStruct((SUBLANES, st.rows), F32),
                   jax.ShapeDtypeStruct((SUBLANES, LANES), F32)],
        scratch_shapes=[pltpu.VMEM((1, LANES), F32)],
        compiler_params=_params("arbitrary"), name="outproj_router",
    )(h, yrec, yatt, w_out, g_ffn, wr, br, tri)


def _swiglu_accumulate(x_ref, w1_ref, w3_ref, w2_ref, acc_s):
    for r in range(0, x_ref.shape[0], FFN_SUB_ROWS):
        x = x_ref[r:r + FFN_SUB_ROWS, :]
        a = jnp.dot(x, w1_ref[...], preferred_element_type=F32)
        b = jnp.dot(x, w3_ref[...], preferred_element_type=F32)
        hh = (a * jax.nn.sigmoid(a) * b).astype(BF16)
        acc_s[r:r + FFN_SUB_ROWS, :] += jnp.dot(hh, w2_ref[...], preferred_element_type=F32)


def _ffn_kernel(x_ref, h_ref, w1_ref, w3_ref, w2_ref, out_ref, acc_s):
    j = pl.program_id(1)

    @pl.when(j == 0)
    def _():
        acc_s[...] = h_ref[...]

    _swiglu_accumulate(x_ref, w1_ref, w3_ref, w2_ref, acc_s)

    @pl.when(j == pl.num_programs(1) - 1)
    def _():
        out_ref[...] = acc_s[...]


def _ffn_dense(st, hn, h, w1, w3, w2):
    RB, FC = FFN_ROWS, FF_CHUNK
    row = lambda i, j: (i, 0)
    return pl.pallas_call(
        _ffn_kernel,
        grid=(st.rows // RB, D_FF // FC),
        in_specs=[pl.BlockSpec((RB, D_MODEL), row), pl.BlockSpec((RB, D_MODEL), row),
                  pl.BlockSpec((D_MODEL, FC), lambda i, j: (0, j)),
                  pl.BlockSpec((D_MODEL, FC), lambda i, j: (0, j)),
                  pl.BlockSpec((FC, D_MODEL), lambda i, j: (j, 0))],
        out_specs=pl.BlockSpec((RB, D_MODEL), row),
        out_shape=jax.ShapeDtypeStruct((st.rows, D_MODEL), F32),
        scratch_shapes=[pltpu.VMEM((RB, D_MODEL), F32)],
        compiler_params=_params("parallel", "arbitrary"),
        name="ffn_dense",
    )(hn, h, w1, w3, w2)


def _row_copy(src_ref, src_row, dst_ref, dst_row, sem):
    return pltpu.make_async_copy(src_ref.at[pl.ds(src_row, 1)], dst_ref.at[pl.ds(dst_row, 1)], sem)


def _dispatch_kernel(pos1_ref, pos2_ref, src_ref, init_ref, dst_ref, sem):
    del init_ref

    def copies(t):
        return (_row_copy(src_ref, t, dst_ref, pos1_ref[t], sem),
                _row_copy(src_ref, t, dst_ref, pos2_ref[t], sem))

    def start(t, _):
        for c in copies(t):
            c.start()
        return 0

    def wait(t, _):
        for c in copies(t):
            c.wait()
        return 0

    lax.fori_loop(0, MOVE_ROWS, start, 0, unroll=8)
    lax.fori_loop(0, MOVE_ROWS, wait, 0, unroll=8)


def _dispatch(st, hn_packed, pos1, pos2, n_slots):
    half = D_MODEL // 2
    idx = pl.BlockSpec((MOVE_ROWS,), lambda i: (i,), memory_space=pltpu.SMEM)
    anywhere = pl.BlockSpec(memory_space=pl.ANY)
    return pl.pallas_call(
        _dispatch_kernel,
        grid=(st.rows // MOVE_ROWS,),
        in_specs=[idx, idx, pl.BlockSpec((MOVE_ROWS, half), lambda i: (i, 0)), anywhere],
        out_specs=anywhere,
        out_shape=jax.ShapeDtypeStruct((n_slots, half), jnp.uint32),
        scratch_shapes=[pltpu.SemaphoreType.DMA],
        input_output_aliases={3: 0},
        compiler_params=_params("arbitrary"),
        name="moe_dispatch",
    )(pos1, pos2, hn_packed, jnp.zeros((n_slots, half), jnp.uint32))


def _moe_ffn_kernel(expert_ref, used_ref, x_ref, w1_ref, w3_ref, w2_ref, y_ref, xb_s, acc_s):
    del expert_ref
    half = D_MODEL // 2
    j = pl.program_id(1)

    @pl.when(pl.program_id(0) < used_ref[0])
    def _():
        @pl.when(j == 0)
        def _():
            lo, hi = _unpack_halves(x_ref[...])
            xb_s[:, :half] = lo.astype(BF16)
            xb_s[:, half:] = hi.astype(BF16)
            acc_s[...] = jnp.zeros_like(acc_s)

        _swiglu_accumulate(xb_s, w1_ref, w3_ref, w2_ref, acc_s)

        @pl.when(j == pl.num_programs(1) - 1)
        def _():
            y_ref[...] = _pack_halves(acc_s[...])


def _moe_ffn(x_sorted, block_expert, n_used, w1, w3, w2):
    RB, FC = FFN_ROWS, FF_CHUNK
    half = D_MODEL // 2
    nj = D_FF // FC
    n_blocks = x_sorted.shape[0] // RB

    def blk(i, used):
        return jnp.minimum(i, used[0] - 1)

    def chunk(i, j, used):
        return jnp.where(i < used[0], j, nj - 1)

    row = lambda i, j, ex, used: (blk(i, used), 0)
    return pl.pallas_call(
        _moe_ffn_kernel,
        grid_spec=pltpu.PrefetchScalarGridSpec(
            num_scalar_prefetch=2,
            grid=(n_blocks, nj),
            in_specs=[pl.BlockSpec((RB, half), row),
                      pl.BlockSpec((None, D_MODEL, FC), lambda i, j, ex, used: (ex[blk(i, used)], 0, chunk(i, j, used))),
                      pl.BlockSpec((None, D_MODEL, FC), lambda i, j, ex, used: (ex[blk(i, used)], 0, chunk(i, j, used))),
                      pl.BlockSpec((None, FC, D_MODEL), lambda i, j, ex, used: (ex[blk(i, used)], chunk(i, j, used), 0))],
            out_specs=pl.BlockSpec((RB, half), row),
            scratch_shapes=[pltpu.VMEM((RB, D_MODEL), BF16), pltpu.VMEM((RB, D_MODEL), F32)],
        ),
        out_shape=jax.ShapeDtypeStruct(x_sorted.shape, jnp.uint32),
        compiler_params=_params("arbitrary", "arbitrary"),
        name="moe_ffn",
    )(block_expert, n_used, x_sorted, w1, w3, w2)


def _combine_kernel(pos1_ref, pos2_ref, h_ref, meta_ref, g_ref, y_ref, out_ref, y1_s, y2_s, sem):
    half = D_MODEL // 2

    def copies(t):
        return (_row_copy(y_ref, pos1_ref[t], y1_s, t, sem), _row_copy(y_ref, pos2_ref[t], y2_s, t, sem))

    def start(t, _):
        for c in copies(t):
            c.start()
        return 0

    def wait(t, _):
        for c in copies(t):
            c.wait()
        return 0

    lax.fori_loop(0, MOVE_ROWS, start, 0, unroll=8)
    lax.fori_loop(0, MOVE_ROWS, wait, 0, unroll=8)
    meta = meta_ref[...]
    g1 = meta[:, META_G1:META_G1 + 1]
    g2 = meta[:, META_G2:META_G2 + 1]
    lo1, hi1 = _unpack_halves(y1_s[...])
    lo2, hi2 = _unpack_halves(y2_s[...])
    lo = h_ref[:, :half] + g1 * lo1 + g2 * lo2
    hi = h_ref[:, half:] + g1 * hi1 + g2 * hi2
    ms = (jnp.sum(lo * lo, axis=-1, keepdims=True) + jnp.sum(hi * hi, axis=-1, keepdims=True)) * (1.0 / D_MODEL)
    inv = lax.rsqrt(ms + EPS)
    out_ref[:, :half] = lo * inv * g_ref[:, :half]
    out_ref[:, half:] = hi * inv * g_ref[:, half:]


def _combine(y_sorted, h, meta, pos1, pos2, g_final, row0, n_rows):
    half = D_MODEL // 2
    b0 = row0 // MOVE_ROWS
    idx = pl.BlockSpec((MOVE_ROWS,), lambda i: (b0 + i,), memory_space=pltpu.SMEM)
    row = lambda i: (b0 + i, 0)
    return pl.pallas_call(
        _combine_kernel,
        grid=(n_rows // MOVE_ROWS,),
        in_specs=[idx, idx, pl.BlockSpec((MOVE_ROWS, D_MODEL), row), pl.BlockSpec((MOVE_ROWS, LANES), row),
                  pl.BlockSpec((1, D_MODEL), lambda i: (0, 0)), pl.BlockSpec(memory_space=pl.ANY)],
        out_specs=pl.BlockSpec((MOVE_ROWS, D_MODEL), lambda i: (i, 0)),
        out_shape=jax.ShapeDtypeStruct((n_rows, D_MODEL), F32),
        scratch_shapes=[pltpu.VMEM((MOVE_ROWS, half), jnp.uint32), pltpu.VMEM((MOVE_ROWS, half), jnp.uint32),
                        pltpu.SemaphoreType.DMA],
        compiler_params=_params("arbitrary"),
        name="moe_combine",
    )(pos1, pos2, h, meta, g_final, y_sorted)


def _routing_tables(metat, counts, n_blocks):
    RB = FFN_ROWS
    cnt = counts[0, :N_EXPERTS].astype(jnp.int32)
    padded = (cnt + RB - 1) // RB * RB
    ends = jnp.cumsum(padded)
    starts = ends - padded
    e1 = metat[META_E1].astype(jnp.int32)
    e2 = metat[META_E2].astype(jnp.int32)
    pos1 = starts[e1] + metat[META_R1].astype(jnp.int32)
    pos2 = starts[e2] + metat[META_R2].astype(jnp.int32)
    block_start = jnp.arange(n_blocks, dtype=jnp.int32) * RB
    block_expert = jnp.sum((block_start[:, None] >= ends[None, :]).astype(jnp.int32), axis=1)
    block_expert = jnp.minimum(block_expert, N_EXPERTS - 1)
    n_used = (ends[-1:] // RB).astype(jnp.int32)
    return pos1, pos2, block_expert, n_used


def _block_diag(w):
    nb, bs, _ = w.shape
    eye = jnp.eye(nb, dtype=w.dtype)
    return jnp.einsum("ncf,nm->ncmf", w, eye).reshape(nb * bs, nb * bs)


def kernel(x_prompt, x_sample, norm_mix, w_in, conv_w, conv_b, w_rgate, b_rgate, w_igate, b_igate,
           lru_lambda, attn_sink, norm_rec_out, norm_attn_out, w_out, norm_ffn, ffn_w1, ffn_w3, ffn_w2,
           moe_router, moe_router_bias, moe_w1, moe_w3, moe_w2, norm_final):
    n_long, s_long, _ = x_prompt.shape
    n_short, s_short, _ = x_sample.shape
    depth = w_in.shape[0]
    assert depth == 2, "layer 0 uses the dense feed-forward, layer 1 the mixture of experts"
    for s in (s_long, s_short):
        assert s % max(ROW_BLOCK, ATT_BLOCK) == 0
    st = Stream(n_long, s_long, n_short, s_short)
    assert st.rows % FFN_ROWS == 0
    assert st.n_long_rows % MOVE_ROWS == 0 and st.rows % MOVE_ROWS == 0

    h = jnp.concatenate([x_prompt.reshape(-1, D_MODEL), x_sample.reshape(-1, D_MODEL)], axis=0)
    tables = _rope_tables(max(s_long, s_short))
    row2 = lambda a: a.reshape(1, -1)

    for l in range(depth):
        xrec, gate, q, k, v = _inproj(st, h, row2(norm_mix[l]), w_in[l].astype(BF16), tables)
        wg = [jnp.concatenate([_block_diag(w_rgate[l, d]), _block_diag(w_igate[l, d])], axis=1).astype(BF16)
              for d in range(2)]
        bg = [row2(jnp.concatenate([b_rgate[l, d], b_igate[l, d]])) for d in range(2)]
        yrec = _recurrent(st, xrec, gate, conv_w[l], row2(conv_b[l]), wg[0], bg[0], wg[1], bg[1],
                          row2(lru_lambda[l, 0]), row2(lru_lambda[l, 1]), row2(norm_rec_out[l]))
        yatt = _attention(st, q, k, v, attn_sink[l], row2(norm_attn_out[l]))
        if l % 2 == 0:
            j = l // 2
            h, hn = _outproj(st, h, yrec, yatt, w_out[l].astype(BF16), row2(norm_ffn[l]))
            h = _ffn_dense(st, hn, h, ffn_w1[j].astype(BF16), ffn_w3[j].astype(BF16), ffn_w2[j].astype(BF16))
        else:
            j = l // 2
            h, hn_packed, meta, metat, counts = _outproj_router(
                st, h, yrec, yatt, w_out[l].astype(BF16), row2(norm_ffn[l]), moe_router[j], moe_router_bias[j])
            n_blocks = TOP_K * st.rows // FFN_ROWS + N_EXPERTS
            pos1, pos2, block_expert, n_used = _routing_tables(metat, counts, n_blocks)
            x_sorted = _dispatch(st, hn_packed, pos1, pos2, n_blocks * FFN_ROWS)
            y_sorted = _moe_ffn(x_sorted, block_expert, n_used, moe_w1[j].astype(BF16), moe_w3[j].astype(BF16),
                                moe_w2[j].astype(BF16))
            outs = [_combine(y_sorted, h, meta, pos1, pos2, row2(norm_final), r0, n)
                    for r0, n in ((0, st.n_long_rows), (st.n_long_rows, st.rows - st.n_long_rows))]
    return (outs[0].reshape(x_prompt.shape), outs[1].reshape(x_sample.shape))
```

```python
import functools

import jax
import jax.numpy as jnp
from jax import lax
from jax.experimental import pallas as pl
from jax.experimental.pallas import tpu as pltpu

F32 = jnp.float32
BF16 = jnp.bfloat16

D_MODEL = 1024
REC_WIDTH = 512
N_REC_BLOCKS = 8
REC_BLOCK = 64
CONV_WIDTH = 4
LRU_C = 8.0
HEAD_DIM = 64
N_Q_HEADS = 8
N_KV_HEADS = 2
Q_PER_KV = 4
ATT_WIDTH = 512
KV_WIDTH = 128
ROT_DIM = 16
ROPE_THETA = 500000.0
WINDOW = 128
D_FF = 3584
N_EXPERTS = 8
EPS = 1e-6
IN_COLS = 2 * REC_WIDTH + ATT_WIDTH + 2 * KV_WIDTH
NEG_INF = -1e30
LOG2_E = 1.4426950408889634
ROW_SUM_ROWS = 16

LANES = 128
SUBLANES = 8
HALO_ROWS = 2 * SUBLANES
VMEM_LIMIT = 56 * 1024 * 1024

ROW_BLOCK = 512
SCAN_STEPS = ROW_BLOCK // SUBLANES
ATT_BLOCK = 512
FFN_ROWS = 512
FF_CHUNK = 512
MOVE_ROWS = 1024
TOP_K = 2


class Stream:
    def __init__(self, n_long, s_long, n_short, s_short):
        self.n_long_rows = n_long * s_long
        self.s_long = s_long
        self.s_short = s_short
        self.rows = n_long * s_long + n_short * s_short

    def seq_pos(self, row0):
        in_long = row0 < self.n_long_rows
        pos = jnp.where(in_long, lax.rem(row0, self.s_long),
                        lax.rem(jnp.maximum(row0 - self.n_long_rows, 0), self.s_short))
        slen = jnp.where(in_long, self.s_long, self.s_short)
        return pos, slen


def _params(*sem):
    return pltpu.CompilerParams(dimension_semantics=sem, vmem_limit_bytes=VMEM_LIMIT)


def _rms(x, g):
    return x * lax.rsqrt(jnp.mean(x * x, axis=-1, keepdims=True) + EPS) * g


def _interleave_matrix():
    p = jnp.arange(ROW_BLOCK)
    src = (p % SUBLANES) * SCAN_STEPS + p // SUBLANES
    return (src[:, None] == jnp.arange(ROW_BLOCK)[None, :]).astype(BF16)


def _split_specs(st, rows_per_block):
    n_long_blocks = st.n_long_rows // rows_per_block
    return [pl.BlockSpec((rows_per_block, D_MODEL), lambda i: (jnp.minimum(i, n_long_blocks - 1), 0)),
            pl.BlockSpec((rows_per_block, D_MODEL), lambda i: (jnp.maximum(i - n_long_blocks, 0), 0))]


def _split_rows(st, rows_per_block, long_ref, short_ref):
    in_long = pl.program_id(0) * rows_per_block < st.n_long_rows
    return jnp.where(in_long, long_ref[...], short_ref[...])


def _inproj_split_kernel(st, long_ref, short_ref, *refs):
    _inproj_body(_split_rows(st, ROW_BLOCK, long_ref, short_ref), *refs)


def _inproj_kernel(x_ref, *refs):
    _inproj_body(x_ref[...], *refs)


def _inproj_body(x, g_ref, w_ref, perm_ref, c_ref, sa_ref, sb_ref,
                 xrec_ref, gate_ref, q_ref, k_ref, v_ref):
    xn = _rms(x, g_ref[...]).astype(BF16)
    proj = jnp.dot(xn, w_ref[...], preferred_element_type=F32)
    rec = jnp.dot(perm_ref[...], proj[:, :2 * REC_WIDTH].astype(BF16), preferred_element_type=F32)
    xrec_ref[...] = rec[:, :REC_WIDTH].astype(BF16)
    gate_ref[...] = rec[:, REC_WIDTH:].astype(BF16)
    c, sa, sb = c_ref[...], sa_ref[...], sb_ref[...]

    def rope(t):
        return (t * c + pltpu.roll(t, LANES - ROT_DIM // 2, 1) * sa
                + pltpu.roll(t, ROT_DIM // 2, 1) * sb)

    q0 = 2 * REC_WIDTH
    scale = HEAD_DIM ** -0.5 * LOG2_E
    for j in range(ATT_WIDTH // LANES):
        t = proj[:, q0 + j * LANES:q0 + (j + 1) * LANES]
        q_ref[:, j * LANES:(j + 1) * LANES] = (rope(t) * scale).astype(BF16)
    k0 = q0 + ATT_WIDTH
    k_ref[...] = rope(proj[:, k0:k0 + KV_WIDTH]).astype(BF16)
    v_ref[...] = proj[:, k0 + KV_WIDTH:k0 + 2 * KV_WIDTH].astype(BF16)


def _rope_tables(s_max):
    inv_freq = ROPE_THETA ** (-jnp.arange(0, ROT_DIM, 2, dtype=F32) / ROT_DIM)
    ang = jnp.arange(s_max, dtype=F32)[:, None] * inv_freq[None, :]
    cos, sin = jnp.cos(ang), jnp.sin(ang)
    half = ROT_DIM // 2
    ones = jnp.ones((s_max, HEAD_DIM - ROT_DIM), F32)
    zeros_h = jnp.zeros((s_max, half), F32)
    zeros_r = jnp.zeros((s_max, HEAD_DIM - ROT_DIM), F32)
    c = jnp.concatenate([cos, cos, ones], axis=1)
    sa = jnp.concatenate([-sin, zeros_h, zeros_r], axis=1)
    sb = jnp.concatenate([zeros_h, sin, zeros_r], axis=1)
    reps = LANES // HEAD_DIM
    return tuple(jnp.tile(t, (1, reps)) for t in (c, sa, sb))


def _inproj(st, h, g, w_in, tables):
    R = ROW_BLOCK
    nblk = st.rows // R

    def pos_blk(i):
        pos, _ = st.seq_pos(i * R)
        return pos // R

    row = lambda i: (i, 0)
    const = lambda i: (0, 0)
    tab = pl.BlockSpec((R, LANES), lambda i: (pos_blk(i), 0))
    outs = [(REC_WIDTH, BF16), (REC_WIDTH, BF16), (ATT_WIDTH, BF16), (KV_WIDTH, BF16), (KV_WIDTH, BF16)]
    if isinstance(h, tuple):
        body, h_specs, h_args = functools.partial(_inproj_split_kernel, st), _split_specs(st, R), list(h)
    else:
        body, h_specs, h_args = _inproj_kernel, [pl.BlockSpec((R, D_MODEL), row)], [h]
    return pl.pallas_call(
        body,
        grid=(nblk,),
        in_specs=h_specs + [pl.BlockSpec((1, D_MODEL), const),
                            pl.BlockSpec((D_MODEL, IN_COLS), const), pl.BlockSpec((R, R), const), tab, tab, tab],
        out_specs=[pl.BlockSpec((R, w), row) for w, _ in outs],
        out_shape=[jax.ShapeDtypeStruct((st.rows, w), dt) for w, dt in outs],
        compiler_params=_params("parallel"),
        name="inproj",
    )(*h_args, g, w_in, _interleave_matrix(), *tables)


def _softplus(x):
    return jnp.maximum(x, 0.0) + jnp.log1p(jnp.exp(-jnp.abs(x)))


def _conv_gates(st, blk, x_ref, xp_ref, xn_ref, cw_ref, cb_ref, wg_ref, bg_ref, lam_ref, a_s, u_s):
    R, S = ROW_BLOCK, SUBLANES
    pos, slen = st.seq_pos(blk * R)
    has_prev = (pos > 0).astype(F32)
    has_next = (pos + R < slen).astype(F32)
    x = x_ref[...].astype(F32)
    sub = lax.broadcasted_iota(jnp.int32, (S, REC_WIDTH), 0)
    prev_rows = xp_ref[...].astype(F32) * has_prev
    next_rows = xn_ref[...].astype(F32) * has_next
    wrap_m1 = jnp.where(sub == 0, prev_rows[2 * S - 1:2 * S], pltpu.roll(x[R - S:], 1, 0))
    wrap_m2 = jnp.where(sub == 0, prev_rows[S - 1:S], pltpu.roll(x[R - 2 * S:R - S], 1, 0))
    wrap_p1 = jnp.where(sub == S - 1, next_rows[0:1], pltpu.roll(x[:S], S - 1, 0))
    taps = (jnp.concatenate([wrap_m2, wrap_m1, x[:R - 2 * S]], axis=0),
            jnp.concatenate([wrap_m1, x[:R - S]], axis=0),
            x,
            jnp.concatenate([x[S:], wrap_p1], axis=0))
    assert len(taps) == CONV_WIDTH and CONV_WIDTH // 2 == 2
    xc = cb_ref[...]
    for t in range(CONV_WIDTH):
        xc = xc + taps[t] * cw_ref[t:t + 1, :]
    pre = jnp.dot(xc.astype(BF16), wg_ref[...], preferred_element_type=F32) + bg_ref[...]
    r = 0.5 * jnp.tanh(0.5 * pre[:, :REC_WIDTH]) + 0.5
    i = 0.5 * jnp.tanh(0.5 * pre[:, REC_WIDTH:]) + 0.5
    a = jnp.exp(r * ((-LRU_C) * _softplus(-lam_ref[...])))
    a_s[...] = a
    u_s[...] = jnp.sqrt(1.0 - a * a) * (i * xc)


def _scan_block(a_s, u_s, out_ref, carry_in, reverse):
    S = SUBLANES
    width = a_s.shape[1]

    def group(j):
        jj = SCAN_STEPS - 1 - j if reverse else j
        return pl.ds(pl.multiple_of(jj * S, S), S)

    def local_step(j, hp):
        h, p = hp
        a = a_s[group(j), :]
        return a * h + u_s[group(j), :], a * p

    h_end, p_end = lax.fori_loop(0, SCAN_STEPS, local_step,
                                 (jnp.zeros((S, width), F32), jnp.ones((S, width), F32)), unroll=8)
    enter = [None] * S
    cur = carry_in
    for s in (range(S - 1, -1, -1) if reverse else range(S)):
        enter[s] = cur
        cur = h_end[s:s + 1] + p_end[s:s + 1] * cur

    def true_step(j, h):
        h = a_s[group(j), :] * h + u_s[group(j), :]
        out_ref[group(j), :] = h
        return h

    lax.fori_loop(0, SCAN_STEPS, true_step, jnp.concatenate(enter, axis=0), unroll=8)
    return cur


def _rec_fwd_kernel(st, x_ref, xp_ref, xn_ref, cw_ref, cb_ref, wg_ref, bg_ref, lam_ref,
                    hf_ref, a_s, u_s, carry_s):
    blk = pl.program_id(0)
    _conv_gates(st, blk, x_ref, xp_ref, xn_ref, cw_ref, cb_ref, wg_ref, bg_ref, lam_ref, a_s, u_s)
    pos, _ = st.seq_pos(blk * ROW_BLOCK)
    carry_in = jnp.where(pos > 0, carry_s[...], 0.0)
    carry_s[...] = _scan_block(a_s, u_s, hf_ref, carry_in, reverse=False)


def _rec_bwd_kernel(st, nblk, x_ref, xp_ref, xn_ref, cw_ref, cb_ref, wg_ref, bg_ref, lam_ref,
                    hf_ref, gate_ref, gout_ref, unperm_ref, y_ref, a_s, u_s, hb_s, carry_s):
    blk = nblk - 1 - pl.program_id(0)
    _conv_gates(st, blk, x_ref, xp_ref, xn_ref, cw_ref, cb_ref, wg_ref, bg_ref, lam_ref, a_s, u_s)
    pos, slen = st.seq_pos(blk * ROW_BLOCK)
    carry_in = jnp.where(pos + ROW_BLOCK < slen, carry_s[...], 0.0)
    carry_s[...] = _scan_block(a_s, u_s, hb_s, carry_in, reverse=True)
    y = (hf_ref[...] + hb_s[...]) * jax.nn.gelu(gate_ref[...].astype(F32), approximate=True)
    y = _rms(y, gout_ref[...]).astype(BF16)
    y_ref[...] = jnp.dot(unperm_ref[...], y, preferred_element_type=F32).astype(BF16)


def _recurrent(st, xrec, gate, conv_w, conv_b, wg_f, bg_f, wg_b, bg_b, lam_f, lam_b, g_rec_out):
    R = ROW_BLOCK
    nblk = st.rows // R
    hpb = R // HALO_ROWS
    n_halo = st.rows // HALO_ROWS
    const = lambda i: (0, 0)

    def specs(order):
        cur = lambda i: (order(i), 0)
        prev = lambda i: (jnp.maximum(order(i) * hpb - 1, 0), 0)
        nxt = lambda i: (jnp.minimum((order(i) + 1) * hpb, n_halo - 1), 0)
        return [pl.BlockSpec((R, REC_WIDTH), cur), pl.BlockSpec((HALO_ROWS, REC_WIDTH), prev),
                pl.BlockSpec((HALO_ROWS, REC_WIDTH), nxt),
                pl.BlockSpec((CONV_WIDTH, REC_WIDTH), const), pl.BlockSpec((1, REC_WIDTH), const),
                pl.BlockSpec((REC_WIDTH, 2 * REC_WIDTH), const), pl.BlockSpec((1, 2 * REC_WIDTH), const),
                pl.BlockSpec((1, REC_WIDTH), const)], cur

    block = (R, REC_WIDTH)
    work = [pltpu.VMEM(block, F32) for _ in range(2)]
    carry = pltpu.VMEM((1, REC_WIDTH), F32)
    in_f, cur_f = specs(lambda i: i)
    hf = pl.pallas_call(
        functools.partial(_rec_fwd_kernel, st),
        grid=(nblk,),
        in_specs=in_f,
        out_specs=pl.BlockSpec(block, cur_f),
        out_shape=jax.ShapeDtypeStruct((st.rows, REC_WIDTH), F32),
        scratch_shapes=work + [carry],
        compiler_params=_params("arbitrary"),
        name="rec_fwd",
    )(xrec, xrec, xrec, conv_w, conv_b, wg_f, bg_f, lam_f)
    in_b, cur_b = specs(lambda i: nblk - 1 - i)
    return pl.pallas_call(
        functools.partial(_rec_bwd_kernel, st, nblk),
        grid=(nblk,),
        in_specs=in_b + [pl.BlockSpec(block, cur_b), pl.BlockSpec(block, cur_b),
                         pl.BlockSpec((1, REC_WIDTH), const), pl.BlockSpec((R, R), const)],
        out_specs=pl.BlockSpec(block, cur_b),
        out_shape=jax.ShapeDtypeStruct((st.rows, REC_WIDTH), BF16),
        scratch_shapes=work + [pltpu.VMEM(block, F32), carry],
        compiler_params=_params("arbitrary"),
        name="rec_bwd",
    )(xrec, xrec, xrec, conv_w, conv_b, wg_b, bg_b, lam_b, hf, gate, g_rec_out, _interleave_matrix().T)


def _attn_kernel(st, sink_ref, q_ref, k_ref, kp_ref, kn_ref, v_ref, vp_ref, vn_ref, g_ref,
                 y_ref, acc_s):
    QB = ATT_BLOCK
    W = WINDOW
    assert LANES == 2 * HEAD_DIM and KV_WIDTH == LANES and N_KV_HEADS == 2 and Q_PER_KV == 4
    pos0, slen = st.seq_pos(pl.program_id(0) * QB)
    kx = jnp.concatenate([kp_ref[...], k_ref[...], kn_ref[...]], axis=0)
    kx_swapped = jnp.concatenate([kx[:, HEAD_DIM:], kx[:, :HEAD_DIM]], axis=1)
    lane = lax.broadcasted_iota(jnp.int32, kx.shape, 1)
    zero = jnp.zeros_like(kx)
    low, high = lane < HEAD_DIM, lane >= HEAD_DIM
    k_sel = ((jnp.where(low, kx, zero), jnp.where(high, kx_swapped, zero)),
             (jnp.where(low, kx_swapped, zero), jnp.where(high, kx, zero)))
    vx = jnp.concatenate([vp_ref[...], v_ref[...], vn_ref[...]], axis=0).astype(F32)
    vt = vx.T.astype(BF16)
    nb = QB // W
    kj = lax.broadcasted_iota(jnp.int32, (W, 2 * W), 0)
    qi = lax.broadcasted_iota(jnp.int32, (W, 2 * W), 1) % W
    before = jnp.where(kj >= qi, 0.0, NEG_INF)
    after = jnp.where(kj <= qi, 0.0, NEG_INF)
    before_first = jnp.where(pos0 > 0, before, NEG_INF)
    after_last = jnp.where(pos0 + QB < slen, after, NEG_INF)
    ones_rows = jnp.ones((ROW_SUM_ROWS, 3 * W), BF16)
    chains = [(b, g, parity) for b in range(nb) for g in range(N_KV_HEADS) for parity in range(2)]
    scores = {}
    for b, g, parity in chains:
        qg = jnp.concatenate([q_ref[b * W:(b + 1) * W, (2 * g + c) * LANES:(2 * g + c + 1) * LANES]
                              for c in range(2)], axis=0)
        kg = k_sel[g][parity][b * W:(b + 3) * W]
        s = lax.dot_general(kg, qg, (((1,), (1,)), ((), ())), preferred_element_type=F32)
        scores[b, g, parity] = jnp.concatenate(
            [s[:W] + (before_first if b == 0 else before), s[W:2 * W],
             s[2 * W:] + (after_last if b == nb - 1 else after)], axis=0)
    probs = {}
    for b, g, parity in chains:
        s = scores[b, g, parity]
        sink = jnp.concatenate([jnp.full((1, W), sink_ref[4 * g + 2 * c + parity] * LOG2_E, F32)
                                for c in range(2)], axis=1)
        m = jnp.maximum(jnp.max(s, axis=0, keepdims=True), sink)
        probs[b, g, parity] = (jnp.exp2(s - m).astype(BF16), jnp.exp2(sink - m))
    outs = {}
    for b, g, parity in chains:
        p, sink_term = probs[b, g, parity]
        vg = jnp.concatenate([vt[g * HEAD_DIM:(g + 1) * HEAD_DIM, b * W:(b + 3) * W], ones_rows], axis=0)
        o = jnp.dot(vg, p, preferred_element_type=F32)
        outs[b, g, parity] = o[:HEAD_DIM] / (o[HEAD_DIM:HEAD_DIM + 1] + sink_term)
    for b in range(nb):
        for g in range(N_KV_HEADS):
            o_t = jnp.concatenate([outs[b, g, 0], outs[b, g, 1]], axis=0).T
            for c in range(2):
                acc_s[b * W:(b + 1) * W, (2 * g + c) * LANES:(2 * g + c + 1) * LANES] = o_t[c * W:(c + 1) * W]
    y_ref[...] = _rms(acc_s[...], g_ref[...]).astype(BF16)


def _attention(st, q, k, v, sink, g_attn_out):
    QB = ATT_BLOCK
    W = WINDOW
    nblk = st.rows // QB
    wpb = QB // W
    n_w = st.rows // W
    cur = lambda i, s: (i, 0)
    prev = lambda i, s: (jnp.maximum(i * wpb - 1, 0), 0)
    nxt = lambda i, s: (jnp.minimum((i + 1) * wpb, n_w - 1), 0)
    kv = [pl.BlockSpec((QB, KV_WIDTH), cur), pl.BlockSpec((W, KV_WIDTH), prev), pl.BlockSpec((W, KV_WIDTH), nxt)]
    return pl.pallas_call(
        functools.partial(_attn_kernel, st),
        grid_spec=pltpu.PrefetchScalarGridSpec(
            num_scalar_prefetch=1,
            grid=(nblk,),
            in_specs=[pl.BlockSpec((QB, ATT_WIDTH), cur)] + kv + kv
                     + [pl.BlockSpec((1, ATT_WIDTH), lambda i, s: (0, 0))],
            out_specs=pl.BlockSpec((QB, ATT_WIDTH), cur),
            scratch_shapes=[pltpu.VMEM((QB, ATT_WIDTH), F32)],
        ),
        out_shape=jax.ShapeDtypeStruct((st.rows, ATT_WIDTH), BF16),
        compiler_params=_params("parallel"),
        name="attention",
    )(sink, q, k, k, k, v, v, v, g_attn_out)


def _outproj_split_kernel(st, long_ref, short_ref, *refs):
    _outproj_body(_split_rows(st, ROW_BLOCK, long_ref, short_ref), *refs)


def _outproj_kernel(h_ref, *refs):
    _outproj_body(h_ref[...], *refs)


def _outproj_body(h_in, yr_ref, ya_ref, w_ref, g_ref, hout_ref, hn_ref):
    y = (jnp.dot(yr_ref[...], w_ref[:REC_WIDTH, :], preferred_element_type=F32)
         + jnp.dot(ya_ref[...], w_ref[REC_WIDTH:, :], preferred_element_type=F32))
    h = h_in + y
    hout_ref[...] = h
    hn_ref[...] = _rms(h, g_ref[...]).astype(BF16)


def _pack_halves(x):
    n = x.shape[1] // 2

    def bf16_bits(v):
        b = lax.bitcast_convert_type(v, jnp.uint32)
        return (b + jnp.uint32(0x7FFF) + ((b >> 16) & jnp.uint32(1))) >> 16

    return bf16_bits(x[:, :n]) | (bf16_bits(x[:, n:]) << 16)


def _unpack_halves(w):
    lo = lax.bitcast_convert_type(w << 16, F32)
    hi = lax.bitcast_convert_type(w & jnp.uint32(0xFFFF0000), F32)
    return lo, hi


META_E1, META_E2, META_R1, META_R2, META_G1, META_G2 = range(6)


def _outproj_router_kernel(h_ref, yr_ref, ya_ref, w_ref, g_ref, wr_ref, br_ref, tri_ref,
                           hout_ref, hnp_ref, meta_ref, metat_ref, counts_ref, carry_s):
    @pl.when(pl.program_id(0) == 0)
    def _():
        carry_s[...] = jnp.zeros_like(carry_s)

    y = (jnp.dot(yr_ref[...], w_ref[:REC_WIDTH, :], preferred_element_type=F32)
         + jnp.dot(ya_ref[...], w_ref[REC_WIDTH:, :], preferred_element_type=F32))
    h = h_ref[...] + y
    hout_ref[...] = h
    hn = _rms(h, g_ref[...])
    hnp_ref[...] = _pack_halves(hn)
    hi = hn.astype(BF16)
    lo = (hn - hi.astype(F32)).astype(BF16)
    a = jnp.dot(hi, wr_ref[...], preferred_element_type=F32)
    b = jnp.dot(lo, wr_ref[...], preferred_element_type=F32)
    logits = a + pltpu.roll(a, LANES - N_EXPERTS, 1) + b + br_ref[...]
    lane = lax.broadcasted_iota(jnp.int32, logits.shape, 1)
    logits = jnp.where(lane < N_EXPERTS, logits, -jnp.inf)
    m1 = jnp.max(logits, axis=-1, keepdims=True)
    i1 = jnp.min(jnp.where(logits == m1, lane, LANES), axis=-1, keepdims=True)
    rest = jnp.where(lane == i1, -jnp.inf, logits)
    m2 = jnp.max(rest, axis=-1, keepdims=True)
    i2 = jnp.min(jnp.where(rest == m2, lane, LANES), axis=-1, keepdims=True)
    e2 = jnp.exp(m2 - m1)
    g1 = 1.0 / (1.0 + e2)
    g2 = e2 * g1
    sel1 = lane == i1
    sel2 = lane == i2
    chosen = jnp.where(sel1 | sel2, 1.0, 0.0)
    incl = jnp.dot(tri_ref[...], chosen.astype(BF16), preferred_element_type=F32)
    rank = incl - chosen + carry_s[...]
    r1 = jnp.sum(jnp.where(sel1, rank, 0.0), axis=-1, keepdims=True)
    r2 = jnp.sum(jnp.where(sel2, rank, 0.0), axis=-1, keepdims=True)
    total = carry_s[...] + incl[ROW_BLOCK - 1:ROW_BLOCK, :]
    carry_s[...] = total
    counts_ref[...] = jnp.broadcast_to(total, counts_ref.shape)
    meta = jnp.zeros_like(logits)
    for ln, val in ((META_E1, i1.astype(F32)), (META_E2, i2.astype(F32)), (META_R1, r1), (META_R2, r2),
                    (META_G1, g1), (META_G2, g2)):
        meta = jnp.where(lane == ln, val, meta)
    meta_ref[...] = meta
    metat_ref[...] = meta.T[:SUBLANES, :]


def _outproj(st, h, yrec, yatt, w_out, g_ffn):
    R = ROW_BLOCK
    row = lambda i: (i, 0)
    const = lambda i: (0, 0)
    if isinstance(h, tuple):
        body, h_specs, h_args = functools.partial(_outproj_split_kernel, st), _split_specs(st, R), list(h)
    else:
        body, h_specs, h_args = _outproj_kernel, [pl.BlockSpec((R, D_MODEL), row)], [h]
    return pl.pallas_call(
        body,
        grid=(st.rows // R,),
        in_specs=h_specs + [pl.BlockSpec((R, REC_WIDTH), row),
                            pl.BlockSpec((R, ATT_WIDTH), row), pl.BlockSpec((D_MODEL, D_MODEL), const),
                            pl.BlockSpec((1, D_MODEL), const)],
        out_specs=[pl.BlockSpec((R, D_MODEL), row), pl.BlockSpec((R, D_MODEL), row)],
        out_shape=[jax.ShapeDtypeStruct((st.rows, D_MODEL), F32), jax.ShapeDtypeStruct((st.rows, D_MODEL), BF16)],
        compiler_params=_params("parallel"), name="outproj",
    )(*h_args, yrec, yatt, w_out, g_ffn)


def _outproj_router(st, h, yrec, yatt, w_out, g_ffn, w_router, b_router):
    R = ROW_BLOCK
    row = lambda i: (i, 0)
    const = lambda i: (0, 0)
    w_hi = w_router.astype(BF16)
    w_lo = (w_router - w_hi.astype(F32)).astype(BF16)
    wr = jnp.zeros((D_MODEL, LANES), BF16).at[:, :N_EXPERTS].set(w_hi).at[:, N_EXPERTS:2 * N_EXPERTS].set(w_lo)
    br = jnp.zeros((1, LANES), F32).at[0, :N_EXPERTS].set(b_router)
    tri = jnp.tril(jnp.ones((R, R), BF16))
    half = D_MODEL // 2
    return pl.pallas_call(
        _outproj_router_kernel,
        grid=(st.rows // R,),
        in_specs=[pl.BlockSpec((R, D_MODEL), row), pl.BlockSpec((R, REC_WIDTH), row),
                  pl.BlockSpec((R, ATT_WIDTH), row), pl.BlockSpec((D_MODEL, D_MODEL), const),
                  pl.BlockSpec((1, D_MODEL), const), pl.BlockSpec((D_MODEL, LANES), const),
                  pl.BlockSpec((1, LANES), const), pl.BlockSpec((R, R), const)],
        out_specs=[pl.BlockSpec((R, D_MODEL), row), pl.BlockSpec((R, half), row), pl.BlockSpec((R, LANES), row),
                   pl.BlockSpec((SUBLANES, R), lambda i: (0, i)), pl.BlockSpec((SUBLANES, LANES), const)],
        out_shape=[jax.ShapeDtypeStruct((st.rows, D_MODEL), F32), jax.ShapeDtypeStruct((st.rows, half), jnp.uint32),
                   jax.ShapeDtypeStruct((st.rows, LANES), F32), jax.ShapeDtypeStruct((SUBLANES, st.rows), F32),
                   jax.ShapeDtypeStruct((SUBLANES, LANES), F32)],
        scratch_shapes=[pltpu.VMEM((1, LANES), F32)],
        compiler_params=_params("arbitrary"), name="outproj_router",
    )(h, yrec, yatt, w_out, g_ffn, wr, br, tri)


def _swiglu(x, acc, w1_ref, w3_ref, w2_ref):
    for c in range(0, D_FF, FF_CHUNK):
        a = jnp.dot(x, w1_ref[:, c:c + FF_CHUNK], preferred_element_type=F32)
        b = jnp.dot(x, w3_ref[:, c:c + FF_CHUNK], preferred_element_type=F32)
        hh = (a * jax.nn.sigmoid(a) * b).astype(BF16)
        acc = acc + jnp.dot(hh, w2_ref[c:c + FF_CHUNK, :], preferred_element_type=F32)
    return acc


def _ffn_kernel(x_ref, h_ref, w1_ref, w3_ref, w2_ref, out_ref):
    out_ref[...] = _swiglu(x_ref[...], h_ref[...], w1_ref, w3_ref, w2_ref)


def _ffn_dense(st, hn, h, w1, w3, w2):
    RB = FFN_ROWS
    row = lambda i: (i, 0)
    const = lambda i: (0, 0)
    resident = pl.Buffered(1)
    return pl.pallas_call(
        _ffn_kernel,
        grid=(st.rows // RB,),
        in_specs=[pl.BlockSpec((RB, D_MODEL), row), pl.BlockSpec((RB, D_MODEL), row),
                  pl.BlockSpec((D_MODEL, D_FF), const, pipeline_mode=resident),
                  pl.BlockSpec((D_MODEL, D_FF), const, pipeline_mode=resident),
                  pl.BlockSpec((D_FF, D_MODEL), const, pipeline_mode=resident)],
        out_specs=pl.BlockSpec((RB, D_MODEL), row),
        out_shape=jax.ShapeDtypeStruct((st.rows, D_MODEL), F32),
        compiler_params=_params("parallel"),
        name="ffn_dense",
    )(hn, h, w1, w3, w2)


def _row_copy(src_ref, src_row, dst_ref, dst_row, sem):
    return pltpu.make_async_copy(src_ref.at[pl.ds(src_row, 1)], dst_ref.at[pl.ds(dst_row, 1)], sem)


def _dispatch_kernel(pos1_ref, pos2_ref, src_ref, init_ref, dst_ref, sem):
    del init_ref

    def copies(t):
        return (_row_copy(src_ref, t, dst_ref, pos1_ref[t], sem),
                _row_copy(src_ref, t, dst_ref, pos2_ref[t], sem))

    def start(t, _):
        for queue, c in enumerate(copies(t)):
            c.start(priority=queue)
        return 0

    def wait(t, _):
        for c in copies(t):
            c.wait()
        return 0

    lax.fori_loop(0, MOVE_ROWS, start, 0, unroll=8)
    lax.fori_loop(0, MOVE_ROWS, wait, 0, unroll=8)


def _dispatch(st, hn_packed, pos1, pos2, n_slots):
    half = D_MODEL // 2
    idx = pl.BlockSpec((MOVE_ROWS,), lambda i: (i,), memory_space=pltpu.SMEM)
    anywhere = pl.BlockSpec(memory_space=pl.ANY)
    return pl.pallas_call(
        _dispatch_kernel,
        grid=(st.rows // MOVE_ROWS,),
        in_specs=[idx, idx, pl.BlockSpec((MOVE_ROWS, half), lambda i: (i, 0)), anywhere],
        out_specs=anywhere,
        out_shape=jax.ShapeDtypeStruct((n_slots, half), jnp.uint32),
        scratch_shapes=[pltpu.SemaphoreType.DMA],
        input_output_aliases={3: 0},
        compiler_params=_params("arbitrary"),
        name="moe_dispatch",
    )(pos1, pos2, hn_packed, jnp.zeros((n_slots, half), jnp.uint32))


def _moe_ffn_kernel(expert_ref, used_ref, x_ref, w1_ref, w3_ref, w2_ref, y_ref):
    del expert_ref

    @pl.when(pl.program_id(0) < used_ref[0])
    def _():
        lo, hi = _unpack_halves(x_ref[...])
        x = jnp.concatenate([lo.astype(BF16), hi.astype(BF16)], axis=1)
        y_ref[...] = _pack_halves(_swiglu(x, jnp.zeros((FFN_ROWS, D_MODEL), F32), w1_ref, w3_ref, w2_ref))


def _moe_ffn(x_sorted, block_expert, n_used, w1, w3, w2):
    RB = FFN_ROWS
    half = D_MODEL // 2
    n_blocks = x_sorted.shape[0] // RB

    def blk(i, used):
        return jnp.minimum(i, used[0] - 1)

    row = lambda i, ex, used: (blk(i, used), 0)
    expert = lambda i, ex, used: (ex[blk(i, used)], 0, 0)
    resident = pl.Buffered(1)
    return pl.pallas_call(
        _moe_ffn_kernel,
        grid_spec=pltpu.PrefetchScalarGridSpec(
            num_scalar_prefetch=2,
            grid=(n_blocks,),
            in_specs=[pl.BlockSpec((RB, half), row),
                      pl.BlockSpec((None, D_MODEL, D_FF), expert, pipeline_mode=resident),
                      pl.BlockSpec((None, D_MODEL, D_FF), expert, pipeline_mode=resident),
                      pl.BlockSpec((None, D_FF, D_MODEL), expert, pipeline_mode=resident)],
            out_specs=pl.BlockSpec((RB, half), row),
        ),
        out_shape=jax.ShapeDtypeStruct(x_sorted.shape, jnp.uint32),
        compiler_params=_params("arbitrary"),
        name="moe_ffn",
    )(block_expert, n_used, x_sorted, w1, w3, w2)


def _combine_kernel(pos1_ref, pos2_ref, h_ref, meta_ref, g_ref, y_ref, out_ref, y1_s, y2_s, sem):
    half = D_MODEL // 2

    def copies(t):
        return (_row_copy(y_ref, pos1_ref[t], y1_s, t, sem), _row_copy(y_ref, pos2_ref[t], y2_s, t, sem))

    def start(t, _):
        for c in copies(t):
            c.start()
        return 0

    def wait(t, _):
        for c in copies(t):
            c.wait()
        return 0

    lax.fori_loop(0, MOVE_ROWS, start, 0, unroll=8)
    lax.fori_loop(0, MOVE_ROWS, wait, 0, unroll=8)
    meta = meta_ref[...]
    g1 = meta[:, META_G1:META_G1 + 1]
    g2 = meta[:, META_G2:META_G2 + 1]
    lo1, hi1 = _unpack_halves(y1_s[...])
    lo2, hi2 = _unpack_halves(y2_s[...])
    lo = h_ref[:, :half] + g1 * lo1 + g2 * lo2
    hi = h_ref[:, half:] + g1 * hi1 + g2 * hi2
    ms = (jnp.sum(lo * lo, axis=-1, keepdims=True) + jnp.sum(hi * hi, axis=-1, keepdims=True)) * (1.0 / D_MODEL)
    inv = lax.rsqrt(ms + EPS)
    out_ref[:, :half] = lo * inv * g_ref[:, :half]
    out_ref[:, half:] = hi * inv * g_ref[:, half:]


def _combine(y_sorted, h, meta, pos1, pos2, g_final, row0, n_rows):
    half = D_MODEL // 2
    b0 = row0 // MOVE_ROWS
    idx = pl.BlockSpec((MOVE_ROWS,), lambda i: (b0 + i,), memory_space=pltpu.SMEM)
    row = lambda i: (b0 + i, 0)
    return pl.pallas_call(
        _combine_kernel,
        grid=(n_rows // MOVE_ROWS,),
        in_specs=[idx, idx, pl.BlockSpec((MOVE_ROWS, D_MODEL), row), pl.BlockSpec((MOVE_ROWS, LANES), row),
                  pl.BlockSpec((1, D_MODEL), lambda i: (0, 0)), pl.BlockSpec(memory_space=pl.ANY)],
        out_specs=pl.BlockSpec((MOVE_ROWS, D_MODEL), lambda i: (i, 0)),
        out_shape=jax.ShapeDtypeStruct((n_rows, D_MODEL), F32),
        scratch_shapes=[pltpu.VMEM((MOVE_ROWS, half), jnp.uint32), pltpu.VMEM((MOVE_ROWS, half), jnp.uint32),
                        pltpu.SemaphoreType.DMA],
        compiler_params=_params("arbitrary"),
        name="moe_combine",
    )(pos1, pos2, h, meta, g_final, y_sorted)


def _routing_tables(metat, counts, n_blocks):
    RB = FFN_ROWS
    cnt = counts[0, :N_EXPERTS].astype(jnp.int32)
    padded = (cnt + RB - 1) // RB * RB
    ends = jnp.cumsum(padded)
    starts = ends - padded
    e1 = metat[META_E1].astype(jnp.int32)
    e2 = metat[META_E2].astype(jnp.int32)
    pos1 = starts[e1] + metat[META_R1].astype(jnp.int32)
    pos2 = starts[e2] + metat[META_R2].astype(jnp.int32)
    block_start = jnp.arange(n_blocks, dtype=jnp.int32) * RB
    block_expert = jnp.sum((block_start[:, None] >= ends[None, :]).astype(jnp.int32), axis=1)
    block_expert = jnp.minimum(block_expert, N_EXPERTS - 1)
    n_used = (ends[-1:] // RB).astype(jnp.int32)
    return pos1, pos2, block_expert, n_used


def _block_diag(w):
    nb, bs, _ = w.shape
    eye = jnp.eye(nb, dtype=w.dtype)
    return jnp.einsum("ncf,nm->ncmf", w, eye).reshape(nb * bs, nb * bs)


def kernel(x_prompt, x_sample, norm_mix, w_in, conv_w, conv_b, w_rgate, b_rgate, w_igate, b_igate,
           lru_lambda, attn_sink, norm_rec_out, norm_attn_out, w_out, norm_ffn, ffn_w1, ffn_w3, ffn_w2,
           moe_router, moe_router_bias, moe_w1, moe_w3, moe_w2, norm_final):
    n_long, s_long, _ = x_prompt.shape
    n_short, s_short, _ = x_sample.shape
    depth = w_in.shape[0]
    assert depth == 2, "layer 0 uses the dense feed-forward, layer 1 the mixture of experts"
    for s in (s_long, s_short):
        assert s % max(ROW_BLOCK, ATT_BLOCK) == 0
    st = Stream(n_long, s_long, n_short, s_short)
    assert st.rows % FFN_ROWS == 0
    assert st.n_long_rows % MOVE_ROWS == 0 and st.rows % MOVE_ROWS == 0

    h = (x_prompt.reshape(-1, D_MODEL), x_sample.reshape(-1, D_MODEL))
    tables = _rope_tables(max(s_long, s_short))
    row2 = lambda a: a.reshape(1, -1)

    for l in range(depth):
        xrec, gate, q, k, v = _inproj(st, h, row2(norm_mix[l]), w_in[l].astype(BF16), tables)
        wg = [jnp.concatenate([_block_diag(w_rgate[l, d]), _block_diag(w_igate[l, d])], axis=1).astype(BF16)
              for d in range(2)]
        bg = [row2(jnp.concatenate([b_rgate[l, d], b_igate[l, d]])) for d in range(2)]
        yrec = _recurrent(st, xrec, gate, conv_w[l], row2(conv_b[l]), wg[0], bg[0], wg[1], bg[1],
                          row2(lru_lambda[l, 0]), row2(lru_lambda[l, 1]), row2(norm_rec_out[l]))
        yatt = _attention(st, q, k, v, attn_sink[l], row2(norm_attn_out[l]))
        if l % 2 == 0:
            j = l // 2
            h, hn = _outproj(st, h, yrec, yatt, w_out[l].astype(BF16), row2(norm_ffn[l]))
            h = _ffn_dense(st, hn, h, ffn_w1[j].astype(BF16), ffn_w3[j].astype(BF16), ffn_w2[j].astype(BF16))
        else:
            j = l // 2
            h, hn_packed, meta, metat, counts = _outproj_router(
                st, h, yrec, yatt, w_out[l].astype(BF16), row2(norm_ffn[l]), moe_router[j], moe_router_bias[j])
            n_blocks = TOP_K * st.rows // FFN_ROWS + N_EXPERTS
            pos1, pos2, block_expert, n_used = _routing_tables(metat, counts, n_blocks)
            x_sorted = _dispatch(st, hn_packed, pos1, pos2, n_blocks * FFN_ROWS)
            y_sorted = _moe_ffn(x_sorted, block_expert, n_used, moe_w1[j].astype(BF16), moe_w3[j].astype(BF16),
                                moe_w2[j].astype(BF16))
            outs = [_combine(y_sorted, h, meta, pos1, pos2, row2(norm_final), r0, n)
                    for r0, n in ((0, st.n_long_rows), (st.n_long_rows, st.rows - st.n_long_rows))]
    return (outs[0].reshape(x_prompt.shape), outs[1].reshape(x_sample.shape))
```

```python
import functools

import jax
import jax.numpy as jnp
from jax import lax
from jax.experimental import pallas as pl
from jax.experimental.pallas import tpu as pltpu

F32 = jnp.float32
BF16 = jnp.bfloat16

D_MODEL = 1024
REC_WIDTH = 512
N_REC_BLOCKS = 8
REC_BLOCK = 64
CONV_WIDTH = 4
LRU_C = 8.0
HEAD_DIM = 64
N_Q_HEADS = 8
N_KV_HEADS = 2
Q_PER_KV = 4
ATT_WIDTH = 512
KV_WIDTH = 128
ROT_DIM = 16
ROPE_THETA = 500000.0
WINDOW = 128
D_FF = 3584
N_EXPERTS = 8
EPS = 1e-6
IN_COLS = 2 * REC_WIDTH + ATT_WIDTH + 2 * KV_WIDTH
NEG_INF = -1e30
LOG2_E = 1.4426950408889634
ROW_SUM_ROWS = 16

LANES = 128
SUBLANES = 8
HALO_ROWS = 2 * SUBLANES
VMEM_LIMIT = 56 * 1024 * 1024

ROW_BLOCK = 512
SCAN_STEPS = ROW_BLOCK // SUBLANES
ATT_BLOCK = 512
FFN_ROWS = 512
FF_CHUNK = 512
MOVE_ROWS = 1024
TOP_K = 2


class Stream:
    def __init__(self, n_long, s_long, n_short, s_short):
        self.n_long_rows = n_long * s_long
        self.s_long = s_long
        self.s_short = s_short
        self.rows = n_long * s_long + n_short * s_short

    def seq_pos(self, row0):
        in_long = row0 < self.n_long_rows
        pos = jnp.where(in_long, lax.rem(row0, self.s_long),
                        lax.rem(jnp.maximum(row0 - self.n_long_rows, 0), self.s_short))
        slen = jnp.where(in_long, self.s_long, self.s_short)
        return pos, slen


def _params(*sem):
    return pltpu.CompilerParams(dimension_semantics=sem, vmem_limit_bytes=VMEM_LIMIT)


def _rms(x, g):
    return x * lax.rsqrt(jnp.mean(x * x, axis=-1, keepdims=True) + EPS) * g


def _interleave_matrix():
    p = jnp.arange(ROW_BLOCK)
    src = (p % SUBLANES) * SCAN_STEPS + p // SUBLANES
    return (src[:, None] == jnp.arange(ROW_BLOCK)[None, :]).astype(BF16)


def _split_specs(st, rows_per_block, order=lambda i: i):
    n_long_blocks = st.n_long_rows // rows_per_block
    return [pl.BlockSpec((rows_per_block, D_MODEL), lambda i: (jnp.minimum(order(i), n_long_blocks - 1), 0)),
            pl.BlockSpec((rows_per_block, D_MODEL), lambda i: (jnp.maximum(order(i) - n_long_blocks, 0), 0))]


def _split_rows(st, rows_per_block, long_ref, short_ref):
    in_long = pl.program_id(0) * rows_per_block < st.n_long_rows
    return jnp.where(in_long, long_ref[...], short_ref[...])


def _inproj_split_kernel(st, long_ref, short_ref, *refs):
    _inproj_body(_split_rows(st, ROW_BLOCK, long_ref, short_ref), *refs)


def _inproj_kernel(x_ref, *refs):
    _inproj_body(x_ref[...], *refs)


def _inproj_body(x, g_ref, w_ref, perm_ref, c_ref, sa_ref, sb_ref,
                 xrec_ref, gate_ref, q_ref, k_ref, v_ref):
    xn = _rms(x, g_ref[...]).astype(BF16)
    proj = jnp.dot(xn, w_ref[...], preferred_element_type=F32)
    rec = jnp.dot(perm_ref[...], proj[:, :2 * REC_WIDTH].astype(BF16), preferred_element_type=F32)
    xrec_ref[...] = rec[:, :REC_WIDTH].astype(BF16)
    gate_ref[...] = rec[:, REC_WIDTH:].astype(BF16)
    c, sa, sb = c_ref[...], sa_ref[...], sb_ref[...]

    def rope(t):
        return (t * c + pltpu.roll(t, LANES - ROT_DIM // 2, 1) * sa
                + pltpu.roll(t, ROT_DIM // 2, 1) * sb)

    q0 = 2 * REC_WIDTH
    scale = HEAD_DIM ** -0.5 * LOG2_E
    for j in range(ATT_WIDTH // LANES):
        t = proj[:, q0 + j * LANES:q0 + (j + 1) * LANES]
        q_ref[:, j * LANES:(j + 1) * LANES] = (rope(t) * scale).astype(BF16)
    k0 = q0 + ATT_WIDTH
    k_ref[...] = rope(proj[:, k0:k0 + KV_WIDTH]).astype(BF16)
    v_ref[...] = proj[:, k0 + KV_WIDTH:k0 + 2 * KV_WIDTH].astype(BF16)


def _rope_tables(s_max):
    inv_freq = ROPE_THETA ** (-jnp.arange(0, ROT_DIM, 2, dtype=F32) / ROT_DIM)
    ang = jnp.arange(s_max, dtype=F32)[:, None] * inv_freq[None, :]
    cos, sin = jnp.cos(ang), jnp.sin(ang)
    half = ROT_DIM // 2
    ones = jnp.ones((s_max, HEAD_DIM - ROT_DIM), F32)
    zeros_h = jnp.zeros((s_max, half), F32)
    zeros_r = jnp.zeros((s_max, HEAD_DIM - ROT_DIM), F32)
    c = jnp.concatenate([cos, cos, ones], axis=1)
    sa = jnp.concatenate([-sin, zeros_h, zeros_r], axis=1)
    sb = jnp.concatenate([zeros_h, sin, zeros_r], axis=1)
    reps = LANES // HEAD_DIM
    return tuple(jnp.tile(t, (1, reps)) for t in (c, sa, sb))


def _inproj(st, h, g, w_in, tables):
    R = ROW_BLOCK
    nblk = st.rows // R

    def pos_blk(i):
        pos, _ = st.seq_pos(i * R)
        return pos // R

    row = lambda i: (i, 0)
    const = lambda i: (0, 0)
    tab = pl.BlockSpec((R, LANES), lambda i: (pos_blk(i), 0))
    outs = [(REC_WIDTH, BF16), (REC_WIDTH, BF16), (ATT_WIDTH, BF16), (KV_WIDTH, BF16), (KV_WIDTH, BF16)]
    if isinstance(h, tuple):
        body, h_specs, h_args = functools.partial(_inproj_split_kernel, st), _split_specs(st, R), list(h)
    else:
        body, h_specs, h_args = _inproj_kernel, [pl.BlockSpec((R, D_MODEL), row)], [h]
    return pl.pallas_call(
        body,
        grid=(nblk,),
        in_specs=h_specs + [pl.BlockSpec((1, D_MODEL), const),
                            pl.BlockSpec((D_MODEL, IN_COLS), const), pl.BlockSpec((R, R), const), tab, tab, tab],
        out_specs=[pl.BlockSpec((R, w), row) for w, _ in outs],
        out_shape=[jax.ShapeDtypeStruct((st.rows, w), dt) for w, dt in outs],
        compiler_params=_params("parallel"),
        name="inproj",
    )(*h_args, g, w_in, _interleave_matrix(), *tables)


def _softplus(x):
    return jnp.maximum(x, 0.0) + jnp.log1p(jnp.exp(-jnp.abs(x)))


def _conv_gates(st, blk, x_ref, xp_ref, xn_ref, cw_ref, cb_ref, wg_ref, bg_ref, lam_ref, a_s, u_s):
    R, S = ROW_BLOCK, SUBLANES
    pos, slen = st.seq_pos(blk * R)
    has_prev = (pos > 0).astype(F32)
    has_next = (pos + R < slen).astype(F32)
    x = x_ref[...].astype(F32)
    sub = lax.broadcasted_iota(jnp.int32, (S, REC_WIDTH), 0)
    prev_rows = xp_ref[...].astype(F32) * has_prev
    next_rows = xn_ref[...].astype(F32) * has_next
    wrap_m1 = jnp.where(sub == 0, prev_rows[2 * S - 1:2 * S], pltpu.roll(x[R - S:], 1, 0))
    wrap_m2 = jnp.where(sub == 0, prev_rows[S - 1:S], pltpu.roll(x[R - 2 * S:R - S], 1, 0))
    wrap_p1 = jnp.where(sub == S - 1, next_rows[0:1], pltpu.roll(x[:S], S - 1, 0))
    taps = (jnp.concatenate([wrap_m2, wrap_m1, x[:R - 2 * S]], axis=0),
            jnp.concatenate([wrap_m1, x[:R - S]], axis=0),
            x,
            jnp.concatenate([x[S:], wrap_p1], axis=0))
    assert len(taps) == CONV_WIDTH and CONV_WIDTH // 2 == 2
    xc = cb_ref[...]
    for t in range(CONV_WIDTH):
        xc = xc + taps[t] * cw_ref[t:t + 1, :]
    half_pre = jnp.dot(xc.astype(BF16), wg_ref[...], preferred_element_type=F32) + bg_ref[...]
    t_r = jnp.tanh(half_pre[:, :REC_WIDTH])
    i = 0.5 * jnp.tanh(half_pre[:, REC_WIDTH:]) + 0.5
    half_k = (0.5 * LOG2_E * -LRU_C) * _softplus(-lam_ref[...])
    a = jnp.exp2(t_r * half_k + half_k)
    a_s[...] = a
    y = 1.0 - a * a
    root = jnp.where(y > 0.0, y * lax.rsqrt(y), 0.0)
    u_s[...] = root * (i * xc)


def _scan_block(a_s, u_s, out_ref, carry_in, reverse):
    S = SUBLANES
    width = a_s.shape[1]

    def group(j):
        jj = SCAN_STEPS - 1 - j if reverse else j
        return pl.ds(pl.multiple_of(jj * S, S), S)

    def local_step(j, hp):
        h, p = hp
        a = a_s[group(j), :]
        return a * h + u_s[group(j), :], a * p

    h_end, p_end = lax.fori_loop(0, SCAN_STEPS, local_step,
                                 (jnp.zeros((S, width), F32), jnp.ones((S, width), F32)), unroll=8)
    enter = [None] * S
    cur = carry_in
    for s in (range(S - 1, -1, -1) if reverse else range(S)):
        enter[s] = cur
        cur = h_end[s:s + 1] + p_end[s:s + 1] * cur

    def true_step(j, h):
        h = a_s[group(j), :] * h + u_s[group(j), :]
        out_ref[group(j), :] = h
        return h

    lax.fori_loop(0, SCAN_STEPS, true_step, jnp.concatenate(enter, axis=0), unroll=8)
    return cur


def _rec_fwd_kernel(st, x_ref, xp_ref, xn_ref, cw_ref, cb_ref, wg_ref, bg_ref, lam_ref,
                    hf_ref, a_s, u_s, carry_s):
    blk = pl.program_id(0)
    _conv_gates(st, blk, x_ref, xp_ref, xn_ref, cw_ref, cb_ref, wg_ref, bg_ref, lam_ref, a_s, u_s)
    pos, _ = st.seq_pos(blk * ROW_BLOCK)
    carry_in = jnp.where(pos > 0, carry_s[...], 0.0)
    carry_s[...] = _scan_block(a_s, u_s, hf_ref, carry_in, reverse=False)


def _rec_bwd_rows(st, blk, x_ref, xp_ref, xn_ref, cw_ref, cb_ref, wg_ref, bg_ref, lam_ref,
                  hf_ref, gate_ref, gout_ref, unperm_ref, a_s, u_s, hb_s, carry_s):
    _conv_gates(st, blk, x_ref, xp_ref, xn_ref, cw_ref, cb_ref, wg_ref, bg_ref, lam_ref, a_s, u_s)
    pos, slen = st.seq_pos(blk * ROW_BLOCK)
    carry_in = jnp.where(pos + ROW_BLOCK < slen, carry_s[...], 0.0)
    carry_s[...] = _scan_block(a_s, u_s, hb_s, carry_in, reverse=True)
    y = (hf_ref[...] + hb_s[...]) * jax.nn.gelu(gate_ref[...].astype(F32), approximate=True)
    y = _rms(y, gout_ref[...]).astype(BF16)
    return jnp.dot(unperm_ref[...], y, preferred_element_type=F32).astype(BF16)


N_REC_BWD_INPUTS = 12


def _rec_specs(st, order):
    R = ROW_BLOCK
    hpb = R // HALO_ROWS
    n_halo = st.rows // HALO_ROWS
    const = lambda i: (0, 0)
    cur = lambda i: (order(i), 0)
    prev = lambda i: (jnp.maximum(order(i) * hpb - 1, 0), 0)
    nxt = lambda i: (jnp.minimum((order(i) + 1) * hpb, n_halo - 1), 0)
    return [pl.BlockSpec((R, REC_WIDTH), cur), pl.BlockSpec((HALO_ROWS, REC_WIDTH), prev),
            pl.BlockSpec((HALO_ROWS, REC_WIDTH), nxt),
            pl.BlockSpec((CONV_WIDTH, REC_WIDTH), const), pl.BlockSpec((1, REC_WIDTH), const),
            pl.BlockSpec((REC_WIDTH, 2 * REC_WIDTH), const), pl.BlockSpec((1, 2 * REC_WIDTH), const),
            pl.BlockSpec((1, REC_WIDTH), const)]


def _recurrent_fwd(st, xrec, conv_w, conv_b, wg_f, bg_f, lam_f):
    R = ROW_BLOCK
    block = (R, REC_WIDTH)
    return pl.pallas_call(
        functools.partial(_rec_fwd_kernel, st),
        grid=(st.rows // R,),
        in_specs=_rec_specs(st, lambda i: i),
        out_specs=pl.BlockSpec(block, lambda i: (i, 0)),
        out_shape=jax.ShapeDtypeStruct((st.rows, REC_WIDTH), F32),
        scratch_shapes=[pltpu.VMEM(block, F32), pltpu.VMEM(block, F32), pltpu.VMEM((1, REC_WIDTH), F32)],
        compiler_params=_params("arbitrary"),
        name="rec_fwd",
    )(xrec, xrec, xrec, conv_w, conv_b, wg_f, bg_f, lam_f)


def _attn_kernel(st, sink_ref, q_ref, k_ref, kp_ref, kn_ref, v_ref, vp_ref, vn_ref, g_ref,
                 y_ref, acc_s):
    QB = ATT_BLOCK
    W = WINDOW
    assert LANES == 2 * HEAD_DIM and KV_WIDTH == LANES and N_KV_HEADS == 2 and Q_PER_KV == 4
    pos0, slen = st.seq_pos(pl.program_id(0) * QB)
    kx = jnp.concatenate([kp_ref[...], k_ref[...], kn_ref[...]], axis=0)
    kx_swapped = jnp.concatenate([kx[:, HEAD_DIM:], kx[:, :HEAD_DIM]], axis=1)
    lane = lax.broadcasted_iota(jnp.int32, kx.shape, 1)
    zero = jnp.zeros_like(kx)
    low, high = lane < HEAD_DIM, lane >= HEAD_DIM
    k_sel = ((jnp.where(low, kx, zero), jnp.where(high, kx_swapped, zero)),
             (jnp.where(low, kx_swapped, zero), jnp.where(high, kx, zero)))
    vx = jnp.concatenate([vp_ref[...], v_ref[...], vn_ref[...]], axis=0).astype(F32)
    vt = vx.T.astype(BF16)
    nb = QB // W
    kj = lax.broadcasted_iota(jnp.int32, (W, 2 * W), 0)
    qi = lax.broadcasted_iota(jnp.int32, (W, 2 * W), 1) % W
    before = jnp.where(kj >= qi, 0.0, NEG_INF)
    after = jnp.where(kj <= qi, 0.0, NEG_INF)
    before_first = jnp.where(pos0 > 0, before, NEG_INF)
    after_last = jnp.where(pos0 + QB < slen, after, NEG_INF)
    ones_rows = jnp.ones((ROW_SUM_ROWS, 3 * W), BF16)
    chains = [(b, g, parity) for b in range(nb) for g in range(N_KV_HEADS) for parity in range(2)]
    scores = {}
    for b, g, parity in chains:
        qg = jnp.concatenate([q_ref[b * W:(b + 1) * W, (2 * g + c) * LANES:(2 * g + c + 1) * LANES]
                              for c in range(2)], axis=0)
        kg = k_sel[g][parity][b * W:(b + 3) * W]
        s = lax.dot_general(kg, qg, (((1,), (1,)), ((), ())), preferred_element_type=F32)
        scores[b, g, parity] = jnp.concatenate(
            [s[:W] + (before_first if b == 0 else before), s[W:2 * W],
             s[2 * W:] + (after_last if b == nb - 1 else after)], axis=0)
    probs = {}
    for b, g, parity in chains:
        s = scores[b, g, parity]
        sink = jnp.concatenate([jnp.full((1, W), sink_ref[4 * g + 2 * c + parity] * LOG2_E, F32)
                                for c in range(2)], axis=1)
        m = jnp.maximum(jnp.max(s, axis=0, keepdims=True), sink)
        probs[b, g, parity] = (jnp.exp2(s - m).astype(BF16), jnp.exp2(sink - m))
    outs = {}
    for b, g, parity in chains:
        p, sink_term = probs[b, g, parity]
        vg = jnp.concatenate([vt[g * HEAD_DIM:(g + 1) * HEAD_DIM, b * W:(b + 3) * W], ones_rows], axis=0)
        o = jnp.dot(vg, p, preferred_element_type=F32)
        outs[b, g, parity] = o[:HEAD_DIM] / (o[HEAD_DIM:HEAD_DIM + 1] + sink_term)
    for b in range(nb):
        for g in range(N_KV_HEADS):
            o_t = jnp.concatenate([outs[b, g, 0], outs[b, g, 1]], axis=0).T
            for c in range(2):
                acc_s[b * W:(b + 1) * W, (2 * g + c) * LANES:(2 * g + c + 1) * LANES] = o_t[c * W:(c + 1) * W]
    y_ref[...] = _rms(acc_s[...], g_ref[...]).astype(BF16)


def _attention(st, q, k, v, sink, g_attn_out):
    QB = ATT_BLOCK
    W = WINDOW
    nblk = st.rows // QB
    wpb = QB // W
    n_w = st.rows // W
    cur = lambda i, s: (i, 0)
    prev = lambda i, s: (jnp.maximum(i * wpb - 1, 0), 0)
    nxt = lambda i, s: (jnp.minimum((i + 1) * wpb, n_w - 1), 0)
    kv = [pl.BlockSpec((QB, KV_WIDTH), cur), pl.BlockSpec((W, KV_WIDTH), prev), pl.BlockSpec((W, KV_WIDTH), nxt)]
    return pl.pallas_call(
        functools.partial(_attn_kernel, st),
        grid_spec=pltpu.PrefetchScalarGridSpec(
            num_scalar_prefetch=1,
            grid=(nblk,),
            in_specs=[pl.BlockSpec((QB, ATT_WIDTH), cur)] + kv + kv
                     + [pl.BlockSpec((1, ATT_WIDTH), lambda i, s: (0, 0))],
            out_specs=pl.BlockSpec((QB, ATT_WIDTH), cur),
            scratch_shapes=[pltpu.VMEM((QB, ATT_WIDTH), F32)],
        ),
        out_shape=jax.ShapeDtypeStruct((st.rows, ATT_WIDTH), BF16),
        compiler_params=_params("parallel"),
        name="attention",
    )(sink, q, k, k, k, v, v, v, g_attn_out)


def _mixer_residual(st, nblk, n_h, refs):
    blk = nblk - 1 - pl.program_id(0)
    h_refs, refs = refs[:n_h], refs[n_h:]
    rec_refs, refs = refs[:N_REC_BWD_INPUTS], refs[N_REC_BWD_INPUTS:]
    ya_ref, w_ref = refs[:2]
    scratch = refs[-4:]
    yrec = _rec_bwd_rows(st, blk, *rec_refs, *scratch)
    if n_h == 1:
        h_in = h_refs[0][...]
    else:
        h_in = jnp.where(blk * ROW_BLOCK < st.n_long_rows, h_refs[0][...], h_refs[1][...])
    h = (h_in + jnp.dot(yrec, w_ref[:REC_WIDTH, :], preferred_element_type=F32)
         + jnp.dot(ya_ref[...], w_ref[REC_WIDTH:, :], preferred_element_type=F32))
    return h, refs[2:-4]


def _outproj_kernel(st, nblk, n_h, *refs):
    h, (g_ref, hout_ref, hn_ref) = _mixer_residual(st, nblk, n_h, refs)
    hout_ref[...] = h
    hn_ref[...] = _rms(h, g_ref[...]).astype(BF16)


def _pack_halves(x):
    n = x.shape[1] // 2
    bits = lax.bitcast_convert_type(x.astype(BF16).astype(F32), jnp.uint32)
    return (bits[:, :n] >> 16) | bits[:, n:]


def _unpack_halves(w):
    lo = lax.bitcast_convert_type(w << 16, F32)
    hi = lax.bitcast_convert_type(w & jnp.uint32(0xFFFF0000), F32)
    return lo, hi


META_E1, META_E2, META_R1, META_R2, META_G1, META_G2 = range(6)


def _outproj_router_kernel(st, nblk, n_h, *refs):
    carry_s = refs[-1]
    h, (g_ref, wr_ref, br_ref, tri_ref, hout_ref, hnp_ref, meta_ref, metat_ref, counts_ref) = _mixer_residual(
        st, nblk, n_h, refs[:-1])

    @pl.when(pl.program_id(0) == 0)
    def _():
        carry_s[...] = jnp.zeros_like(carry_s)

    hout_ref[...] = h
    hn = _rms(h, g_ref[...])
    hnp_ref[...] = _pack_halves(hn)
    hi = hn.astype(BF16)
    lo = (hn - hi.astype(F32)).astype(BF16)
    a = jnp.dot(hi, wr_ref[...], preferred_element_type=F32)
    b = jnp.dot(lo, wr_ref[...], preferred_element_type=F32)
    logits = a + pltpu.roll(a, LANES - N_EXPERTS, 1) + b + br_ref[...]
    lane = lax.broadcasted_iota(jnp.int32, logits.shape, 1)
    logits = jnp.where(lane < N_EXPERTS, logits, -jnp.inf)
    m1 = jnp.max(logits, axis=-1, keepdims=True)
    i1 = jnp.min(jnp.where(logits == m1, lane, LANES), axis=-1, keepdims=True)
    rest = jnp.where(lane == i1, -jnp.inf, logits)
    m2 = jnp.max(rest, axis=-1, keepdims=True)
    i2 = jnp.min(jnp.where(rest == m2, lane, LANES), axis=-1, keepdims=True)
    e2 = jnp.exp(m2 - m1)
    g1 = 1.0 / (1.0 + e2)
    g2 = e2 * g1
    sel1 = lane == i1
    sel2 = lane == i2
    chosen = jnp.where(sel1 | sel2, 1.0, 0.0)
    incl = jnp.dot(tri_ref[...], chosen.astype(BF16), preferred_element_type=F32)
    rank = incl - chosen + carry_s[...]
    r1 = jnp.sum(jnp.where(sel1, rank, 0.0), axis=-1, keepdims=True)
    r2 = jnp.sum(jnp.where(sel2, rank, 0.0), axis=-1, keepdims=True)
    total = carry_s[...] + incl[ROW_BLOCK - 1:ROW_BLOCK, :]
    carry_s[...] = total
    counts_ref[...] = jnp.broadcast_to(total, counts_ref.shape)
    meta = jnp.zeros_like(logits)
    for ln, val in ((META_E1, i1.astype(F32)), (META_E2, i2.astype(F32)), (META_R1, r1), (META_R2, r2),
                    (META_G1, g1), (META_G2, g2)):
        meta = jnp.where(lane == ln, val, meta)
    meta_ref[...] = meta
    metat_ref[...] = meta.T[:SUBLANES, :]


def _mixer_operands(st, h, rec_bwd_args, yatt, w_out):
    R = ROW_BLOCK
    nblk = st.rows // R
    order = lambda i: nblk - 1 - i
    cur = lambda i: (order(i), 0)
    const = lambda i: (0, 0)
    if isinstance(h, tuple):
        h_specs, h_args = _split_specs(st, R, order), list(h)
    else:
        h_specs, h_args = [pl.BlockSpec((R, D_MODEL), cur)], [h]
    block = (R, REC_WIDTH)
    specs = (h_specs + _rec_specs(st, order)
             + [pl.BlockSpec(block, cur), pl.BlockSpec(block, cur), pl.BlockSpec((1, REC_WIDTH), const),
                pl.BlockSpec((R, R), const)]
             + [pl.BlockSpec((R, ATT_WIDTH), cur), pl.BlockSpec((D_MODEL, D_MODEL), const)])
    args = h_args + list(rec_bwd_args) + [_interleave_matrix().T, yatt, w_out]
    assert len(rec_bwd_args) + 1 == N_REC_BWD_INPUTS
    scratch = [pltpu.VMEM(block, F32) for _ in range(3)] + [pltpu.VMEM((1, REC_WIDTH), F32)]
    return nblk, len(h_args), cur, specs, args, scratch


def _outproj(st, h, rec_bwd_args, yatt, w_out, g_ffn):
    R = ROW_BLOCK
    nblk, n_h, cur, specs, args, scratch = _mixer_operands(st, h, rec_bwd_args, yatt, w_out)
    return pl.pallas_call(
        functools.partial(_outproj_kernel, st, nblk, n_h),
        grid=(nblk,),
        in_specs=specs + [pl.BlockSpec((1, D_MODEL), lambda i: (0, 0))],
        out_specs=[pl.BlockSpec((R, D_MODEL), cur), pl.BlockSpec((R, D_MODEL), cur)],
        out_shape=[jax.ShapeDtypeStruct((st.rows, D_MODEL), F32), jax.ShapeDtypeStruct((st.rows, D_MODEL), BF16)],
        scratch_shapes=scratch,
        compiler_params=_params("arbitrary"), name="outproj",
    )(*args, g_ffn)


def _outproj_router(st, h, rec_bwd_args, yatt, w_out, g_ffn, w_router, b_router):
    R = ROW_BLOCK
    const = lambda i: (0, 0)
    nblk, n_h, cur, specs, args, scratch = _mixer_operands(st, h, rec_bwd_args, yatt, w_out)
    w_hi = w_router.astype(BF16)
    w_lo = (w_router - w_hi.astype(F32)).astype(BF16)
    wr = jnp.zeros((D_MODEL, LANES), BF16).at[:, :N_EXPERTS].set(w_hi).at[:, N_EXPERTS:2 * N_EXPERTS].set(w_lo)
    br = jnp.zeros((1, LANES), F32).at[0, :N_EXPERTS].set(b_router)
    tri = jnp.tril(jnp.ones((R, R), BF16))
    half = D_MODEL // 2
    return pl.pallas_call(
        functools.partial(_outproj_router_kernel, st, nblk, n_h),
        grid=(nblk,),
        in_specs=specs + [pl.BlockSpec((1, D_MODEL), const), pl.BlockSpec((D_MODEL, LANES), const),
                          pl.BlockSpec((1, LANES), const), pl.BlockSpec((R, R), const)],
        out_specs=[pl.BlockSpec((R, D_MODEL), cur), pl.BlockSpec((R, half), cur), pl.BlockSpec((R, LANES), cur),
                   pl.BlockSpec((SUBLANES, R), lambda i: (0, nblk - 1 - i)), pl.BlockSpec((SUBLANES, LANES), const)],
        out_shape=[jax.ShapeDtypeStruct((st.rows, D_MODEL), F32), jax.ShapeDtypeStruct((st.rows, half), jnp.uint32),
                   jax.ShapeDtypeStruct((st.rows, LANES), F32), jax.ShapeDtypeStruct((SUBLANES, st.rows), F32),
                   jax.ShapeDtypeStruct((SUBLANES, LANES), F32)],
        scratch_shapes=scratch + [pltpu.VMEM((1, LANES), F32)],
        compiler_params=_params("arbitrary"), name="outproj_router",
    )(*args, g_ffn, wr, br, tri)


def _swiglu(x, acc, w1_ref, w3_ref, w2_ref):
    for c in range(0, D_FF, FF_CHUNK):
        a = jnp.dot(x, w1_ref[:, c:c + FF_CHUNK], preferred_element_type=F32)
        b = jnp.dot(x, w3_ref[:, c:c + FF_CHUNK], preferred_element_type=F32)
        hh = (a * jax.nn.sigmoid(a) * b).astype(BF16)
        acc = acc + jnp.dot(hh, w2_ref[c:c + FF_CHUNK, :], preferred_element_type=F32)
    return acc


def _ffn_kernel(x_ref, h_ref, w1_ref, w3_ref, w2_ref, out_ref):
    out_ref[...] = _swiglu(x_ref[...], h_ref[...], w1_ref, w3_ref, w2_ref)


def _ffn_dense(st, hn, h, w1, w3, w2):
    RB = FFN_ROWS
    row = lambda i: (i, 0)
    const = lambda i: (0, 0)
    resident = pl.Buffered(1)
    return pl.pallas_call(
        _ffn_kernel,
        grid=(st.rows // RB,),
        in_specs=[pl.BlockSpec((RB, D_MODEL), row), pl.BlockSpec((RB, D_MODEL), row),
                  pl.BlockSpec((D_MODEL, D_FF), const, pipeline_mode=resident),
                  pl.BlockSpec((D_MODEL, D_FF), const, pipeline_mode=resident),
                  pl.BlockSpec((D_FF, D_MODEL), const, pipeline_mode=resident)],
        out_specs=pl.BlockSpec((RB, D_MODEL), row),
        out_shape=jax.ShapeDtypeStruct((st.rows, D_MODEL), F32),
        compiler_params=_params("parallel"),
        name="ffn_dense",
    )(hn, h, w1, w3, w2)


def _row_copy(src_ref, src_row, dst_ref, dst_row, sem):
    return pltpu.make_async_copy(src_ref.at[pl.ds(src_row, 1)], dst_ref.at[pl.ds(dst_row, 1)], sem)


def _dispatch_kernel(pad_lo_ref, pad_hi_ref, pos1_ref, pos2_ref, src_ref, dst_ref, zero_s, sem):
    @pl.when(pl.program_id(0) == 0)
    def _():
        zero_s[...] = jnp.zeros_like(zero_s)
        for e in range(N_EXPERTS):
            def fill(r, _):
                return _row_copy(zero_s, 0, dst_ref, r, sem)

            def start_fill(r, _):
                fill(r, _).start()
                return 0

            def wait_fill(r, _):
                fill(r, _).wait()
                return 0

            lax.fori_loop(pad_lo_ref[e], pad_hi_ref[e], start_fill, 0)
            lax.fori_loop(pad_lo_ref[e], pad_hi_ref[e], wait_fill, 0)

    def copies(t):
        return (_row_copy(src_ref, t, dst_ref, pos1_ref[t], sem),
                _row_copy(src_ref, t, dst_ref, pos2_ref[t], sem))

    def start(t, _):
        for queue, c in enumerate(copies(t)):
            c.start(priority=queue)
        return 0

    def wait(t, _):
        for c in copies(t):
            c.wait()
        return 0

    lax.fori_loop(0, MOVE_ROWS, start, 0, unroll=8)
    lax.fori_loop(0, MOVE_ROWS, wait, 0, unroll=8)


def _dispatch(st, hn_packed, pos1, pos2, pad_lo, pad_hi, n_slots):
    half = D_MODEL // 2
    idx = pl.BlockSpec((MOVE_ROWS,), lambda i, lo, hi: (i,), memory_space=pltpu.SMEM)
    return pl.pallas_call(
        _dispatch_kernel,
        grid_spec=pltpu.PrefetchScalarGridSpec(
            num_scalar_prefetch=2,
            grid=(st.rows // MOVE_ROWS,),
            in_specs=[idx, idx, pl.BlockSpec((MOVE_ROWS, half), lambda i, lo, hi: (i, 0))],
            out_specs=pl.BlockSpec(memory_space=pl.ANY),
            scratch_shapes=[pltpu.VMEM((SUBLANES, half), jnp.uint32), pltpu.SemaphoreType.DMA],
        ),
        out_shape=jax.ShapeDtypeStruct((n_slots, half), jnp.uint32),
        compiler_params=_params("arbitrary"),
        name="moe_dispatch",
    )(pad_lo, pad_hi, pos1, pos2, hn_packed)


def _moe_ffn_kernel(expert_ref, used_ref, x_ref, w1_ref, w3_ref, w2_ref, y_ref):
    del expert_ref

    @pl.when(pl.program_id(0) < used_ref[0])
    def _():
        lo, hi = _unpack_halves(x_ref[...])
        x = jnp.concatenate([lo.astype(BF16), hi.astype(BF16)], axis=1)
        y_ref[...] = _pack_halves(_swiglu(x, jnp.zeros((FFN_ROWS, D_MODEL), F32), w1_ref, w3_ref, w2_ref))


def _moe_ffn(x_sorted, block_expert, n_used, w1, w3, w2):
    RB = FFN_ROWS
    half = D_MODEL // 2
    n_blocks = x_sorted.shape[0] // RB

    def blk(i, used):
        return jnp.minimum(i, used[0] - 1)

    row = lambda i, ex, used: (blk(i, used), 0)
    expert = lambda i, ex, used: (ex[blk(i, used)], 0, 0)
    resident = pl.Buffered(1)
    return pl.pallas_call(
        _moe_ffn_kernel,
        grid_spec=pltpu.PrefetchScalarGridSpec(
            num_scalar_prefetch=2,
            grid=(n_blocks,),
            in_specs=[pl.BlockSpec((RB, half), row),
                      pl.BlockSpec((None, D_MODEL, D_FF), expert, pipeline_mode=resident),
                      pl.BlockSpec((None, D_MODEL, D_FF), expert, pipeline_mode=resident),
                      pl.BlockSpec((None, D_FF, D_MODEL), expert, pipeline_mode=resident)],
            out_specs=pl.BlockSpec((RB, half), row),
        ),
        out_shape=jax.ShapeDtypeStruct(x_sorted.shape, jnp.uint32),
        compiler_params=_params("arbitrary"),
        name="moe_ffn",
    )(block_expert, n_used, x_sorted, w1, w3, w2)


def _combine_kernel(pos1_ref, pos2_ref, h_ref, meta_ref, g_ref, y_ref, out_ref, y1_s, y2_s, sem):
    half = D_MODEL // 2

    def copies(t):
        return (_row_copy(y_ref, pos1_ref[t], y1_s, t, sem), _row_copy(y_ref, pos2_ref[t], y2_s, t, sem))

    def start(t, _):
        for c in copies(t):
            c.start()
        return 0

    def wait(t, _):
        for c in copies(t):
            c.wait()
        return 0

    lax.fori_loop(0, MOVE_ROWS, start, 0, unroll=8)
    lax.fori_loop(0, MOVE_ROWS, wait, 0, unroll=8)
    meta = meta_ref[...]
    g1 = meta[:, META_G1:META_G1 + 1]
    g2 = meta[:, META_G2:META_G2 + 1]
    lo1, hi1 = _unpack_halves(y1_s[...])
    lo2, hi2 = _unpack_halves(y2_s[...])
    lo = h_ref[:, :half] + g1 * lo1 + g2 * lo2
    hi = h_ref[:, half:] + g1 * hi1 + g2 * hi2
    ms = (jnp.sum(lo * lo, axis=-1, keepdims=True) + jnp.sum(hi * hi, axis=-1, keepdims=True)) * (1.0 / D_MODEL)
    inv = lax.rsqrt(ms + EPS)
    out_ref[:, :half] = lo * inv * g_ref[:, :half]
    out_ref[:, half:] = hi * inv * g_ref[:, half:]


def _combine(y_sorted, h, meta, pos1, pos2, g_final, row0, n_rows):
    half = D_MODEL // 2
    b0 = row0 // MOVE_ROWS
    idx = pl.BlockSpec((MOVE_ROWS,), lambda i: (b0 + i,), memory_space=pltpu.SMEM)
    row = lambda i: (b0 + i, 0)
    return pl.pallas_call(
        _combine_kernel,
        grid=(n_rows // MOVE_ROWS,),
        in_specs=[idx, idx, pl.BlockSpec((MOVE_ROWS, D_MODEL), row), pl.BlockSpec((MOVE_ROWS, LANES), row),
                  pl.BlockSpec((1, D_MODEL), lambda i: (0, 0)), pl.BlockSpec(memory_space=pl.ANY)],
        out_specs=pl.BlockSpec((MOVE_ROWS, D_MODEL), lambda i: (i, 0)),
        out_shape=jax.ShapeDtypeStruct((n_rows, D_MODEL), F32),
        scratch_shapes=[pltpu.VMEM((MOVE_ROWS, half), jnp.uint32), pltpu.VMEM((MOVE_ROWS, half), jnp.uint32),
                        pltpu.SemaphoreType.DMA],
        compiler_params=_params("arbitrary"),
        name="moe_combine",
    )(pos1, pos2, h, meta, g_final, y_sorted)


def _routing_tables(metat, counts, n_blocks):
    RB = FFN_ROWS
    cnt = counts[0, :N_EXPERTS].astype(jnp.int32)
    padded = (cnt + RB - 1) // RB * RB
    ends = jnp.cumsum(padded)
    starts = ends - padded
    e1 = metat[META_E1].astype(jnp.int32)
    e2 = metat[META_E2].astype(jnp.int32)
    pos1 = starts[e1] + metat[META_R1].astype(jnp.int32)
    pos2 = starts[e2] + metat[META_R2].astype(jnp.int32)
    block_start = jnp.arange(n_blocks, dtype=jnp.int32) * RB
    block_expert = jnp.sum((block_start[:, None] >= ends[None, :]).astype(jnp.int32), axis=1)
    block_expert = jnp.minimum(block_expert, N_EXPERTS - 1)
    n_used = (ends[-1:] // RB).astype(jnp.int32)
    return pos1, pos2, block_expert, n_used, starts + cnt, ends


def _block_diag(w):
    nb, bs, _ = w.shape
    eye = jnp.eye(nb, dtype=w.dtype)
    return jnp.einsum("ncf,nm->ncmf", w, eye).reshape(nb * bs, nb * bs)


def kernel(x_prompt, x_sample, norm_mix, w_in, conv_w, conv_b, w_rgate, b_rgate, w_igate, b_igate,
           lru_lambda, attn_sink, norm_rec_out, norm_attn_out, w_out, norm_ffn, ffn_w1, ffn_w3, ffn_w2,
           moe_router, moe_router_bias, moe_w1, moe_w3, moe_w2, norm_final):
    n_long, s_long, _ = x_prompt.shape
    n_short, s_short, _ = x_sample.shape
    depth = w_in.shape[0]
    assert depth == 2, "layer 0 uses the dense feed-forward, layer 1 the mixture of experts"
    for s in (s_long, s_short):
        assert s % max(ROW_BLOCK, ATT_BLOCK) == 0
    st = Stream(n_long, s_long, n_short, s_short)
    assert st.rows % FFN_ROWS == 0
    assert st.n_long_rows % MOVE_ROWS == 0 and st.rows % MOVE_ROWS == 0

    h = (x_prompt.reshape(-1, D_MODEL), x_sample.reshape(-1, D_MODEL))
    tables = _rope_tables(max(s_long, s_short))
    row2 = lambda a: a.reshape(1, -1)

    for l in range(depth):
        xrec, gate, q, k, v = _inproj(st, h, row2(norm_mix[l]), w_in[l].astype(BF16), tables)
        wg = [(0.5 * jnp.concatenate([_block_diag(w_rgate[l, d]), _block_diag(w_igate[l, d])], axis=1)).astype(BF16)
              for d in range(2)]
        bg = [row2(0.5 * jnp.concatenate([b_rgate[l, d], b_igate[l, d]])) for d in range(2)]
        cw, cb = conv_w[l], row2(conv_b[l])
        hf = _recurrent_fwd(st, xrec, cw, cb, wg[0], bg[0], row2(lru_lambda[l, 0]))
        rec_bwd_args = (xrec, xrec, xrec, cw, cb, wg[1], bg[1], row2(lru_lambda[l, 1]), hf, gate,
                        row2(norm_rec_out[l]))
        yatt = _attention(st, q, k, v, attn_sink[l], row2(norm_attn_out[l]))
        if l % 2 == 0:
            j = l // 2
            h, hn = _outproj(st, h, rec_bwd_args, yatt, w_out[l].astype(BF16), row2(norm_ffn[l]))
            h = _ffn_dense(st, hn, h, ffn_w1[j].astype(BF16), ffn_w3[j].astype(BF16), ffn_w2[j].astype(BF16))
        else:
            j = l // 2
            h, hn_packed, meta, metat, counts = _outproj_router(
                st, h, rec_bwd_args, yatt, w_out[l].astype(BF16), row2(norm_ffn[l]), moe_router[j],
                moe_router_bias[j])
            n_blocks = TOP_K * st.rows // FFN_ROWS + N_EXPERTS
            pos1, pos2, block_expert, n_used, pad_lo, pad_hi = _routing_tables(metat, counts, n_blocks)
            x_sorted = _dispatch(st, hn_packed, pos1, pos2, pad_lo, pad_hi, n_blocks * FFN_ROWS)
            y_sorted = _moe_ffn(x_sorted, block_expert, n_used, moe_w1[j].astype(BF16), moe_w3[j].astype(BF16),
                                moe_w2[j].astype(BF16))
            outs = [_combine(y_sorted, h, meta, pos1, pos2, row2(norm_final), r0, n)
                    for r0, n in ((0, st.n_long_rows), (st.n_long_rows, st.rows - st.n_long_rows))]
    return (outs[0].reshape(x_prompt.shape), outs[1].reshape(x_sample.shape))
```

```python
import functools

import jax
import jax.numpy as jnp
from jax import lax
from jax.experimental import pallas as pl
from jax.experimental.pallas import tpu as pltpu

F32 = jnp.float32
BF16 = jnp.bfloat16

D_MODEL = 1024
REC_WIDTH = 512
N_REC_BLOCKS = 8
REC_BLOCK = 64
CONV_WIDTH = 4
LRU_C = 8.0
HEAD_DIM = 64
N_Q_HEADS = 8
N_KV_HEADS = 2
Q_PER_KV = 4
ATT_WIDTH = 512
KV_WIDTH = 128
ROT_DIM = 16
ROPE_THETA = 500000.0
WINDOW = 128
D_FF = 3584
N_EXPERTS = 8
EPS = 1e-6
IN_COLS = 2 * REC_WIDTH + ATT_WIDTH + 2 * KV_WIDTH
NEG_INF = -1e30
LOG2_E = 1.4426950408889634
ROW_SUM_ROWS = 16

LANES = 128
SUBLANES = 8
HALO_ROWS = 2 * SUBLANES
VMEM_LIMIT = 56 * 1024 * 1024

ROW_BLOCK = 512
SCAN_STEPS = ROW_BLOCK // SUBLANES
ATT_BLOCK = 512
FFN_ROWS = 512
FF_CHUNK = 512
MOVE_ROWS = 1024
TOP_K = 2


class Stream:
    def __init__(self, n_long, s_long, n_short, s_short):
        self.n_long_rows = n_long * s_long
        self.s_long = s_long
        self.s_short = s_short
        self.rows = n_long * s_long + n_short * s_short

    def seq_pos(self, row0):
        in_long = row0 < self.n_long_rows
        pos = jnp.where(in_long, lax.rem(row0, self.s_long),
                        lax.rem(jnp.maximum(row0 - self.n_long_rows, 0), self.s_short))
        slen = jnp.where(in_long, self.s_long, self.s_short)
        return pos, slen


def _params(*sem):
    return pltpu.CompilerParams(dimension_semantics=sem, vmem_limit_bytes=VMEM_LIMIT)


def _rms(x, g):
    return x * lax.rsqrt(jnp.mean(x * x, axis=-1, keepdims=True) + EPS) * g


def _interleave_matrix():
    p = jnp.arange(ROW_BLOCK)
    src = (p % SUBLANES) * SCAN_STEPS + p // SUBLANES
    return (src[:, None] == jnp.arange(ROW_BLOCK)[None, :]).astype(BF16)


def _split_specs(st, rows_per_block, order=lambda i: i):
    n_long_blocks = st.n_long_rows // rows_per_block
    return [pl.BlockSpec((rows_per_block, D_MODEL), lambda i: (jnp.minimum(order(i), n_long_blocks - 1), 0)),
            pl.BlockSpec((rows_per_block, D_MODEL), lambda i: (jnp.maximum(order(i) - n_long_blocks, 0), 0))]


def _split_rows(st, rows_per_block, long_ref, short_ref):
    in_long = pl.program_id(0) * rows_per_block < st.n_long_rows
    return jnp.where(in_long, long_ref[...], short_ref[...])


def _inproj_split_kernel(st, long_ref, short_ref, *refs):
    _inproj_body(_split_rows(st, ROW_BLOCK, long_ref, short_ref), *refs)


def _inproj_kernel(x_ref, *refs):
    _inproj_body(x_ref[...], *refs)


def _inproj_body(x, g_ref, w_ref, perm_ref, c_ref, sa_ref, sb_ref,
                 xrec_ref, gate_ref, q_ref, k_ref, v_ref):
    xn = _rms(x, g_ref[...]).astype(BF16)
    proj = jnp.dot(xn, w_ref[...], preferred_element_type=F32)
    rec = jnp.dot(perm_ref[...], proj[:, :2 * REC_WIDTH].astype(BF16), preferred_element_type=F32)
    xrec_ref[...] = rec[:, :REC_WIDTH].astype(BF16)
    gate_ref[...] = rec[:, REC_WIDTH:].astype(BF16)
    c, sa, sb = c_ref[...], sa_ref[...], sb_ref[...]

    def rope(t):
        return (t * c + pltpu.roll(t, LANES - ROT_DIM // 2, 1) * sa
                + pltpu.roll(t, ROT_DIM // 2, 1) * sb)

    q0 = 2 * REC_WIDTH
    scale = HEAD_DIM ** -0.5 * LOG2_E
    for j in range(ATT_WIDTH // LANES):
        t = proj[:, q0 + j * LANES:q0 + (j + 1) * LANES]
        q_ref[:, j * LANES:(j + 1) * LANES] = (rope(t) * scale).astype(BF16)
    k0 = q0 + ATT_WIDTH
    k_ref[...] = rope(proj[:, k0:k0 + KV_WIDTH]).astype(BF16)
    v_ref[...] = proj[:, k0 + KV_WIDTH:k0 + 2 * KV_WIDTH].astype(BF16)


def _rope_tables(s_max):
    inv_freq = ROPE_THETA ** (-jnp.arange(0, ROT_DIM, 2, dtype=F32) / ROT_DIM)
    ang = jnp.arange(s_max, dtype=F32)[:, None] * inv_freq[None, :]
    cos, sin = jnp.cos(ang), jnp.sin(ang)
    half = ROT_DIM // 2
    ones = jnp.ones((s_max, HEAD_DIM - ROT_DIM), F32)
    zeros_h = jnp.zeros((s_max, half), F32)
    zeros_r = jnp.zeros((s_max, HEAD_DIM - ROT_DIM), F32)
    c = jnp.concatenate([cos, cos, ones], axis=1)
    sa = jnp.concatenate([-sin, zeros_h, zeros_r], axis=1)
    sb = jnp.concatenate([zeros_h, sin, zeros_r], axis=1)
    reps = LANES // HEAD_DIM
    return tuple(jnp.tile(t, (1, reps)) for t in (c, sa, sb))


def _inproj(st, h, g, w_in, tables):
    R = ROW_BLOCK
    nblk = st.rows // R

    def pos_blk(i):
        pos, _ = st.seq_pos(i * R)
        return pos // R

    row = lambda i: (i, 0)
    const = lambda i: (0, 0)
    tab = pl.BlockSpec((R, LANES), lambda i: (pos_blk(i), 0))
    outs = [(REC_WIDTH, BF16), (REC_WIDTH, BF16), (ATT_WIDTH, BF16), (KV_WIDTH, BF16), (KV_WIDTH, BF16)]
    if isinstance(h, tuple):
        body, h_specs, h_args = functools.partial(_inproj_split_kernel, st), _split_specs(st, R), list(h)
    else:
        body, h_specs, h_args = _inproj_kernel, [pl.BlockSpec((R, D_MODEL), row)], [h]
    return pl.pallas_call(
        body,
        grid=(nblk,),
        in_specs=h_specs + [pl.BlockSpec((1, D_MODEL), const),
                            pl.BlockSpec((D_MODEL, IN_COLS), const), pl.BlockSpec((R, R), const), tab, tab, tab],
        out_specs=[pl.BlockSpec((R, w), row) for w, _ in outs],
        out_shape=[jax.ShapeDtypeStruct((st.rows, w), dt) for w, dt in outs],
        compiler_params=_params("parallel"),
        name="inproj",
    )(*h_args, g, w_in, _interleave_matrix(), *tables)


def _softplus(x):
    return jnp.maximum(x, 0.0) + jnp.log1p(jnp.exp(-jnp.abs(x)))


def _conv_gates(st, blk, x_ref, xp_ref, xn_ref, cw_ref, cb_ref, wg_ref, bg_ref, lam_ref, a_s, u_s):
    R, S = ROW_BLOCK, SUBLANES
    pos, slen = st.seq_pos(blk * R)
    has_prev = (pos > 0).astype(F32)
    has_next = (pos + R < slen).astype(F32)
    x = x_ref[...].astype(F32)
    sub = lax.broadcasted_iota(jnp.int32, (S, REC_WIDTH), 0)
    prev_rows = xp_ref[...].astype(F32) * has_prev
    next_rows = xn_ref[...].astype(F32) * has_next
    wrap_m1 = jnp.where(sub == 0, prev_rows[2 * S - 1:2 * S], pltpu.roll(x[R - S:], 1, 0))
    wrap_m2 = jnp.where(sub == 0, prev_rows[S - 1:S], pltpu.roll(x[R - 2 * S:R - S], 1, 0))
    wrap_p1 = jnp.where(sub == S - 1, next_rows[0:1], pltpu.roll(x[:S], S - 1, 0))
    taps = (jnp.concatenate([wrap_m2, wrap_m1, x[:R - 2 * S]], axis=0),
            jnp.concatenate([wrap_m1, x[:R - S]], axis=0),
            x,
            jnp.concatenate([x[S:], wrap_p1], axis=0))
    assert len(taps) == CONV_WIDTH and CONV_WIDTH // 2 == 2
    xc = cb_ref[...]
    for t in range(CONV_WIDTH):
        xc = xc + taps[t] * cw_ref[t:t + 1, :]
    half_pre = jnp.dot(xc.astype(BF16), wg_ref[...], preferred_element_type=F32) + bg_ref[...]
    t_r = jnp.tanh(half_pre[:, :REC_WIDTH])
    i = 0.5 * jnp.tanh(half_pre[:, REC_WIDTH:]) + 0.5
    half_k = (0.5 * LOG2_E * -LRU_C) * _softplus(-lam_ref[...])
    a = jnp.exp2(t_r * half_k + half_k)
    a_s[...] = a
    y = 1.0 - a * a
    root = jnp.where(y > 0.0, y * lax.rsqrt(y), 0.0)
    u_s[...] = root * (i * xc)


def _scan_block(a_s, u_s, out_ref, carry_in, reverse):
    S = SUBLANES
    width = a_s.shape[1]

    def group(j):
        jj = SCAN_STEPS - 1 - j if reverse else j
        return pl.ds(pl.multiple_of(jj * S, S), S)

    def local_step(j, hp):
        h, p = hp
        a = a_s[group(j), :]
        return a * h + u_s[group(j), :], a * p

    h_end, p_end = lax.fori_loop(0, SCAN_STEPS, local_step,
                                 (jnp.zeros((S, width), F32), jnp.ones((S, width), F32)), unroll=8)
    enter = [None] * S
    cur = carry_in
    for s in (range(S - 1, -1, -1) if reverse else range(S)):
        enter[s] = cur
        cur = h_end[s:s + 1] + p_end[s:s + 1] * cur

    def true_step(j, h):
        h = a_s[group(j), :] * h + u_s[group(j), :]
        out_ref[group(j), :] = h
        return h

    lax.fori_loop(0, SCAN_STEPS, true_step, jnp.concatenate(enter, axis=0), unroll=8)
    return cur


def _rec_fwd_kernel(st, x_ref, xp_ref, xn_ref, cw_ref, cb_ref, wg_ref, bg_ref, lam_ref,
                    hf_ref, a_s, u_s, carry_s):
    blk = pl.program_id(0)
    _conv_gates(st, blk, x_ref, xp_ref, xn_ref, cw_ref, cb_ref, wg_ref, bg_ref, lam_ref, a_s, u_s)
    pos, _ = st.seq_pos(blk * ROW_BLOCK)
    carry_in = jnp.where(pos > 0, carry_s[...], 0.0)
    carry_s[...] = _scan_block(a_s, u_s, hf_ref, carry_in, reverse=False)


def _rec_bwd_rows(st, blk, x_ref, xp_ref, xn_ref, cw_ref, cb_ref, wg_ref, bg_ref, lam_ref,
                  hf_ref, gate_ref, gout_ref, unperm_ref, a_s, u_s, hb_s, carry_s):
    _conv_gates(st, blk, x_ref, xp_ref, xn_ref, cw_ref, cb_ref, wg_ref, bg_ref, lam_ref, a_s, u_s)
    pos, slen = st.seq_pos(blk * ROW_BLOCK)
    carry_in = jnp.where(pos + ROW_BLOCK < slen, carry_s[...], 0.0)
    carry_s[...] = _scan_block(a_s, u_s, hb_s, carry_in, reverse=True)
    y = (hf_ref[...] + hb_s[...]) * jax.nn.gelu(gate_ref[...].astype(F32), approximate=True)
    y = _rms(y, gout_ref[...]).astype(BF16)
    return jnp.dot(unperm_ref[...], y, preferred_element_type=F32).astype(BF16)


N_REC_BWD_INPUTS = 12


def _rec_specs(st, order):
    R = ROW_BLOCK
    hpb = R // HALO_ROWS
    n_halo = st.rows // HALO_ROWS
    const = lambda i: (0, 0)
    cur = lambda i: (order(i), 0)
    prev = lambda i: (jnp.maximum(order(i) * hpb - 1, 0), 0)
    nxt = lambda i: (jnp.minimum((order(i) + 1) * hpb, n_halo - 1), 0)
    return [pl.BlockSpec((R, REC_WIDTH), cur), pl.BlockSpec((HALO_ROWS, REC_WIDTH), prev),
            pl.BlockSpec((HALO_ROWS, REC_WIDTH), nxt),
            pl.BlockSpec((CONV_WIDTH, REC_WIDTH), const), pl.BlockSpec((1, REC_WIDTH), const),
            pl.BlockSpec((REC_WIDTH, 2 * REC_WIDTH), const), pl.BlockSpec((1, 2 * REC_WIDTH), const),
            pl.BlockSpec((1, REC_WIDTH), const)]


def _recurrent_fwd(st, xrec, conv_w, conv_b, wg_f, bg_f, lam_f):
    R = ROW_BLOCK
    block = (R, REC_WIDTH)
    return pl.pallas_call(
        functools.partial(_rec_fwd_kernel, st),
        grid=(st.rows // R,),
        in_specs=_rec_specs(st, lambda i: i),
        out_specs=pl.BlockSpec(block, lambda i: (i, 0)),
        out_shape=jax.ShapeDtypeStruct((st.rows, REC_WIDTH), F32),
        scratch_shapes=[pltpu.VMEM(block, F32), pltpu.VMEM(block, F32), pltpu.VMEM((1, REC_WIDTH), F32)],
        compiler_params=_params("arbitrary"),
        name="rec_fwd",
    )(xrec, xrec, xrec, conv_w, conv_b, wg_f, bg_f, lam_f)


def _attn_kernel(st, sink_ref, q_ref, k_ref, kp_ref, kn_ref, v_ref, vp_ref, vn_ref, g_ref,
                 y_ref, acc_s):
    QB = ATT_BLOCK
    W = WINDOW
    assert LANES == 2 * HEAD_DIM and KV_WIDTH == LANES and N_KV_HEADS == 2 and Q_PER_KV == 4
    pos0, slen = st.seq_pos(pl.program_id(0) * QB)
    kx = jnp.concatenate([kp_ref[...], k_ref[...], kn_ref[...]], axis=0)
    kx_swapped = jnp.concatenate([kx[:, HEAD_DIM:], kx[:, :HEAD_DIM]], axis=1)
    lane = lax.broadcasted_iota(jnp.int32, kx.shape, 1)
    zero = jnp.zeros_like(kx)
    low, high = lane < HEAD_DIM, lane >= HEAD_DIM
    k_sel = ((jnp.where(low, kx, zero), jnp.where(high, kx_swapped, zero)),
             (jnp.where(low, kx_swapped, zero), jnp.where(high, kx, zero)))
    vx = jnp.concatenate([vp_ref[...], v_ref[...], vn_ref[...]], axis=0).astype(F32)
    vt = vx.T.astype(BF16)
    nb = QB // W
    kj = lax.broadcasted_iota(jnp.int32, (W, 2 * W), 0)
    qi = lax.broadcasted_iota(jnp.int32, (W, 2 * W), 1) % W
    before = jnp.where(kj >= qi, 0.0, NEG_INF)
    after = jnp.where(kj <= qi, 0.0, NEG_INF)
    before_first = jnp.where(pos0 > 0, before, NEG_INF)
    after_last = jnp.where(pos0 + QB < slen, after, NEG_INF)
    ones_rows = jnp.ones((ROW_SUM_ROWS, 3 * W), BF16)
    chains = [(b, g, parity) for b in range(nb) for g in range(N_KV_HEADS) for parity in range(2)]
    scores = {}
    for b, g, parity in chains:
        qg = jnp.concatenate([q_ref[b * W:(b + 1) * W, (2 * g + c) * LANES:(2 * g + c + 1) * LANES]
                              for c in range(2)], axis=0)
        kg = k_sel[g][parity][b * W:(b + 3) * W]
        s = lax.dot_general(kg, qg, (((1,), (1,)), ((), ())), preferred_element_type=F32)
        scores[b, g, parity] = jnp.concatenate(
            [s[:W] + (before_first if b == 0 else before), s[W:2 * W],
             s[2 * W:] + (after_last if b == nb - 1 else after)], axis=0)
    probs = {}
    for b, g, parity in chains:
        s = scores[b, g, parity]
        sink = jnp.concatenate([jnp.full((1, W), sink_ref[4 * g + 2 * c + parity] * LOG2_E, F32)
                                for c in range(2)], axis=1)
        m = jnp.maximum(jnp.max(s, axis=0, keepdims=True), sink)
        probs[b, g, parity] = (jnp.exp2(s - m).astype(BF16), jnp.exp2(sink - m))
    outs = {}
    for b, g, parity in chains:
        p, sink_term = probs[b, g, parity]
        vg = jnp.concatenate([vt[g * HEAD_DIM:(g + 1) * HEAD_DIM, b * W:(b + 3) * W], ones_rows], axis=0)
        o = jnp.dot(vg, p, preferred_element_type=F32)
        outs[b, g, parity] = o[:HEAD_DIM] / (o[HEAD_DIM:HEAD_DIM + 1] + sink_term)
    for b in range(nb):
        for g in range(N_KV_HEADS):
            o_t = jnp.concatenate([outs[b, g, 0], outs[b, g, 1]], axis=0).T
            for c in range(2):
                acc_s[b * W:(b + 1) * W, (2 * g + c) * LANES:(2 * g + c + 1) * LANES] = o_t[c * W:(c + 1) * W]
    y_ref[...] = _rms(acc_s[...], g_ref[...]).astype(BF16)


def _attention(st, q, k, v, sink, g_attn_out):
    QB = ATT_BLOCK
    W = WINDOW
    nblk = st.rows // QB
    wpb = QB // W
    n_w = st.rows // W
    cur = lambda i, s: (i, 0)
    prev = lambda i, s: (jnp.maximum(i * wpb - 1, 0), 0)
    nxt = lambda i, s: (jnp.minimum((i + 1) * wpb, n_w - 1), 0)
    kv = [pl.BlockSpec((QB, KV_WIDTH), cur), pl.BlockSpec((W, KV_WIDTH), prev), pl.BlockSpec((W, KV_WIDTH), nxt)]
    return pl.pallas_call(
        functools.partial(_attn_kernel, st),
        grid_spec=pltpu.PrefetchScalarGridSpec(
            num_scalar_prefetch=1,
            grid=(nblk,),
            in_specs=[pl.BlockSpec((QB, ATT_WIDTH), cur)] + kv + kv
                     + [pl.BlockSpec((1, ATT_WIDTH), lambda i, s: (0, 0))],
            out_specs=pl.BlockSpec((QB, ATT_WIDTH), cur),
            scratch_shapes=[pltpu.VMEM((QB, ATT_WIDTH), F32)],
        ),
        out_shape=jax.ShapeDtypeStruct((st.rows, ATT_WIDTH), BF16),
        compiler_params=_params("parallel"),
        name="attention",
    )(sink, q, k, k, k, v, v, v, g_attn_out)


def _mixer_residual(st, nblk, n_h, refs):
    blk = nblk - 1 - pl.program_id(0)
    h_refs, refs = refs[:n_h], refs[n_h:]
    rec_refs, refs = refs[:N_REC_BWD_INPUTS], refs[N_REC_BWD_INPUTS:]
    ya_ref, w_ref = refs[:2]
    scratch = refs[-4:]
    yrec = _rec_bwd_rows(st, blk, *rec_refs, *scratch)
    if n_h == 1:
        h_in = h_refs[0][...]
    else:
        h_in = jnp.where(blk * ROW_BLOCK < st.n_long_rows, h_refs[0][...], h_refs[1][...])
    h = (h_in + jnp.dot(yrec, w_ref[:REC_WIDTH, :], preferred_element_type=F32)
         + jnp.dot(ya_ref[...], w_ref[REC_WIDTH:, :], preferred_element_type=F32))
    return h, refs[2:-4]


def _outproj_kernel(st, nblk, n_h, *refs):
    h, (g_ref, hout_ref, hn_ref) = _mixer_residual(st, nblk, n_h, refs)
    hout_ref[...] = h
    hn_ref[...] = _rms(h, g_ref[...]).astype(BF16)


def _pack_halves(x):
    n = x.shape[1] // 2
    bits = lax.bitcast_convert_type(x.astype(BF16).astype(F32), jnp.uint32)
    return (bits[:, :n] >> 16) | bits[:, n:]


def _unpack_halves(w):
    lo = lax.bitcast_convert_type(w << 16, F32)
    hi = lax.bitcast_convert_type(w & jnp.uint32(0xFFFF0000), F32)
    return lo, hi


META_E1, META_E2, META_R1, META_R2, META_G1, META_G2 = range(6)


def _outproj_router_kernel(st, nblk, n_h, *refs):
    carry_s = refs[-1]
    h, (g_ref, wr_ref, br_ref, tri_ref, hout_ref, hnp_ref, meta_ref, metat_ref, counts_ref) = _mixer_residual(
        st, nblk, n_h, refs[:-1])

    @pl.when(pl.program_id(0) == 0)
    def _():
        carry_s[...] = jnp.zeros_like(carry_s)

    hout_ref[...] = h
    hn = _rms(h, g_ref[...])
    hnp_ref[...] = _pack_halves(hn)
    hi = hn.astype(BF16)
    lo = (hn - hi.astype(F32)).astype(BF16)
    a = jnp.dot(hi, wr_ref[...], preferred_element_type=F32)
    b = jnp.dot(lo, wr_ref[...], preferred_element_type=F32)
    logits = a + pltpu.roll(a, LANES - N_EXPERTS, 1) + b + br_ref[...]
    lane = lax.broadcasted_iota(jnp.int32, logits.shape, 1)
    logits = jnp.where(lane < N_EXPERTS, logits, -jnp.inf)
    m1 = jnp.max(logits, axis=-1, keepdims=True)
    i1 = jnp.min(jnp.where(logits == m1, lane, LANES), axis=-1, keepdims=True)
    rest = jnp.where(lane == i1, -jnp.inf, logits)
    m2 = jnp.max(rest, axis=-1, keepdims=True)
    i2 = jnp.min(jnp.where(rest == m2, lane, LANES), axis=-1, keepdims=True)
    e2 = jnp.exp(m2 - m1)
    g1 = 1.0 / (1.0 + e2)
    g2 = e2 * g1
    sel1 = lane == i1
    sel2 = lane == i2
    chosen = jnp.where(sel1 | sel2, 1.0, 0.0)
    incl = jnp.dot(tri_ref[...], chosen.astype(BF16), preferred_element_type=F32)
    rank = incl - chosen + carry_s[...]
    r1 = jnp.sum(jnp.where(sel1, rank, 0.0), axis=-1, keepdims=True)
    r2 = jnp.sum(jnp.where(sel2, rank, 0.0), axis=-1, keepdims=True)
    total = carry_s[...] + incl[ROW_BLOCK - 1:ROW_BLOCK, :]
    carry_s[...] = total
    counts_ref[...] = jnp.broadcast_to(total, counts_ref.shape)
    meta = jnp.zeros_like(logits)
    for ln, val in ((META_E1, i1.astype(F32)), (META_E2, i2.astype(F32)), (META_R1, r1), (META_R2, r2),
                    (META_G1, g1), (META_G2, g2)):
        meta = jnp.where(lane == ln, val, meta)
    meta_ref[...] = meta
    metat_ref[...] = meta.T[:SUBLANES, :]


def _mixer_operands(st, h, rec_bwd_args, yatt, w_out):
    R = ROW_BLOCK
    nblk = st.rows // R
    order = lambda i: nblk - 1 - i
    cur = lambda i: (order(i), 0)
    const = lambda i: (0, 0)
    if isinstance(h, tuple):
        h_specs, h_args = _split_specs(st, R, order), list(h)
    else:
        h_specs, h_args = [pl.BlockSpec((R, D_MODEL), cur)], [h]
    block = (R, REC_WIDTH)
    specs = (h_specs + _rec_specs(st, order)
             + [pl.BlockSpec(block, cur), pl.BlockSpec(block, cur), pl.BlockSpec((1, REC_WIDTH), const),
                pl.BlockSpec((R, R), const)]
             + [pl.BlockSpec((R, ATT_WIDTH), cur), pl.BlockSpec((D_MODEL, D_MODEL), const)])
    args = h_args + list(rec_bwd_args) + [_interleave_matrix().T, yatt, w_out]
    assert len(rec_bwd_args) + 1 == N_REC_BWD_INPUTS
    scratch = [pltpu.VMEM(block, F32) for _ in range(3)] + [pltpu.VMEM((1, REC_WIDTH), F32)]
    return nblk, len(h_args), cur, specs, args, scratch


def _outproj(st, h, rec_bwd_args, yatt, w_out, g_ffn):
    R = ROW_BLOCK
    nblk, n_h, cur, specs, args, scratch = _mixer_operands(st, h, rec_bwd_args, yatt, w_out)
    return pl.pallas_call(
        functools.partial(_outproj_kernel, st, nblk, n_h),
        grid=(nblk,),
        in_specs=specs + [pl.BlockSpec((1, D_MODEL), lambda i: (0, 0))],
        out_specs=[pl.BlockSpec((R, D_MODEL), cur), pl.BlockSpec((R, D_MODEL), cur)],
        out_shape=[jax.ShapeDtypeStruct((st.rows, D_MODEL), F32), jax.ShapeDtypeStruct((st.rows, D_MODEL), BF16)],
        scratch_shapes=scratch,
        compiler_params=_params("arbitrary"), name="outproj",
    )(*args, g_ffn)


def _outproj_router(st, h, rec_bwd_args, yatt, w_out, g_ffn, w_router, b_router):
    R = ROW_BLOCK
    const = lambda i: (0, 0)
    nblk, n_h, cur, specs, args, scratch = _mixer_operands(st, h, rec_bwd_args, yatt, w_out)
    w_hi = w_router.astype(BF16)
    w_lo = (w_router - w_hi.astype(F32)).astype(BF16)
    wr = jnp.zeros((D_MODEL, LANES), BF16).at[:, :N_EXPERTS].set(w_hi).at[:, N_EXPERTS:2 * N_EXPERTS].set(w_lo)
    br = jnp.zeros((1, LANES), F32).at[0, :N_EXPERTS].set(b_router)
    tri = jnp.tril(jnp.ones((R, R), BF16))
    half = D_MODEL // 2
    return pl.pallas_call(
        functools.partial(_outproj_router_kernel, st, nblk, n_h),
        grid=(nblk,),
        in_specs=specs + [pl.BlockSpec((1, D_MODEL), const), pl.BlockSpec((D_MODEL, LANES), const),
                          pl.BlockSpec((1, LANES), const), pl.BlockSpec((R, R), const)],
        out_specs=[pl.BlockSpec((R, D_MODEL), cur), pl.BlockSpec((R, half), cur), pl.BlockSpec((R, LANES), cur),
                   pl.BlockSpec((SUBLANES, R), lambda i: (0, nblk - 1 - i)), pl.BlockSpec((SUBLANES, LANES), const)],
        out_shape=[jax.ShapeDtypeStruct((st.rows, D_MODEL), F32), jax.ShapeDtypeStruct((st.rows, half), jnp.uint32),
                   jax.ShapeDtypeStruct((st.rows, LANES), F32), jax.ShapeDtypeStruct((SUBLANES, st.rows), F32),
                   jax.ShapeDtypeStruct((SUBLANES, LANES), F32)],
        scratch_shapes=scratch + [pltpu.VMEM((1, LANES), F32)],
        compiler_params=_params("arbitrary"), name="outproj_router",
    )(*args, g_ffn, wr, br, tri)


def _swiglu(x, acc, w1_ref, w3_ref, w2_ref):
    for c in range(0, D_FF, FF_CHUNK):
        a = jnp.dot(x, w1_ref[:, c:c + FF_CHUNK], preferred_element_type=F32)
        b = jnp.dot(x, w3_ref[:, c:c + FF_CHUNK], preferred_element_type=F32)
        hh = (a * jax.nn.sigmoid(a) * b).astype(BF16)
        acc = acc + jnp.dot(hh, w2_ref[c:c + FF_CHUNK, :], preferred_element_type=F32)
    return acc


def _ffn_kernel(x_ref, h_ref, w1_ref, w3_ref, w2_ref, out_ref):
    out_ref[...] = _swiglu(x_ref[...], h_ref[...], w1_ref, w3_ref, w2_ref)


def _ffn_dense(st, hn, h, w1, w3, w2):
    RB = FFN_ROWS
    row = lambda i: (i, 0)
    const = lambda i: (0, 0)
    resident = pl.Buffered(1)
    return pl.pallas_call(
        _ffn_kernel,
        grid=(st.rows // RB,),
        in_specs=[pl.BlockSpec((RB, D_MODEL), row), pl.BlockSpec((RB, D_MODEL), row),
                  pl.BlockSpec((D_MODEL, D_FF), const, pipeline_mode=resident),
                  pl.BlockSpec((D_MODEL, D_FF), const, pipeline_mode=resident),
                  pl.BlockSpec((D_FF, D_MODEL), const, pipeline_mode=resident)],
        out_specs=pl.BlockSpec((RB, D_MODEL), row),
        out_shape=jax.ShapeDtypeStruct((st.rows, D_MODEL), F32),
        compiler_params=_params("parallel"),
        name="ffn_dense",
    )(hn, h, w1, w3, w2)


def _row_copy(src_ref, src_row, dst_ref, dst_row, sem):
    return pltpu.make_async_copy(src_ref.at[pl.ds(src_row, 1)], dst_ref.at[pl.ds(dst_row, 1)], sem)


def _dispatch_kernel(pad_lo_ref, pad_hi_ref, pos1_ref, pos2_ref, src_ref, dst_ref, zero_s, sem):
    @pl.when(pl.program_id(0) == 0)
    def _():
        zero_s[...] = jnp.zeros_like(zero_s)
        for e in range(N_EXPERTS):
            def fill(r, _):
                return _row_copy(zero_s, 0, dst_ref, r, sem)

            def start_fill(r, _):
                fill(r, _).start()
                return 0

            def wait_fill(r, _):
                fill(r, _).wait()
                return 0

            lax.fori_loop(pad_lo_ref[e], pad_hi_ref[e], start_fill, 0)
            lax.fori_loop(pad_lo_ref[e], pad_hi_ref[e], wait_fill, 0)

    def copies(t):
        return (_row_copy(src_ref, t, dst_ref, pos1_ref[t], sem),
                _row_copy(src_ref, t, dst_ref, pos2_ref[t], sem))

    def start(t, _):
        for queue, c in enumerate(copies(t)):
            c.start(priority=queue)
        return 0

    def wait(t, _):
        for c in copies(t):
            c.wait()
        return 0

    lax.fori_loop(0, MOVE_ROWS, start, 0, unroll=8)
    lax.fori_loop(0, MOVE_ROWS, wait, 0, unroll=8)


def _dispatch(st, hn_packed, pos1, pos2, pad_lo, pad_hi, n_slots):
    half = D_MODEL // 2
    idx = pl.BlockSpec((MOVE_ROWS,), lambda i, lo, hi: (i,), memory_space=pltpu.SMEM)
    return pl.pallas_call(
        _dispatch_kernel,
        grid_spec=pltpu.PrefetchScalarGridSpec(
            num_scalar_prefetch=2,
            grid=(st.rows // MOVE_ROWS,),
            in_specs=[idx, idx, pl.BlockSpec((MOVE_ROWS, half), lambda i, lo, hi: (i, 0))],
            out_specs=pl.BlockSpec(memory_space=pl.ANY),
            scratch_shapes=[pltpu.VMEM((SUBLANES, half), jnp.uint32), pltpu.SemaphoreType.DMA],
        ),
        out_shape=jax.ShapeDtypeStruct((n_slots, half), jnp.uint32),
        compiler_params=_params("arbitrary"),
        name="moe_dispatch",
    )(pad_lo, pad_hi, pos1, pos2, hn_packed)


def _moe_ffn_kernel(expert_ref, used_ref, x_ref, w1_ref, w3_ref, w2_ref, y_ref):
    del expert_ref

    @pl.when(pl.program_id(0) < used_ref[0])
    def _():
        lo, hi = _unpack_halves(x_ref[...])
        x = jnp.concatenate([lo.astype(BF16), hi.astype(BF16)], axis=1)
        y_ref[...] = _pack_halves(_swiglu(x, jnp.zeros((FFN_ROWS, D_MODEL), F32), w1_ref, w3_ref, w2_ref))


def _moe_ffn(x_sorted, block_expert, n_used, w1, w3, w2):
    RB = FFN_ROWS
    half = D_MODEL // 2
    n_blocks = x_sorted.shape[0] // RB

    def blk(i, used):
        return jnp.minimum(i, used[0] - 1)

    row = lambda i, ex, used: (blk(i, used), 0)
    expert = lambda i, ex, used: (ex[blk(i, used)], 0, 0)
    resident = pl.Buffered(1)
    return pl.pallas_call(
        _moe_ffn_kernel,
        grid_spec=pltpu.PrefetchScalarGridSpec(
            num_scalar_prefetch=2,
            grid=(n_blocks,),
            in_specs=[pl.BlockSpec((RB, half), row),
                      pl.BlockSpec((None, D_MODEL, D_FF), expert, pipeline_mode=resident),
                      pl.BlockSpec((None, D_MODEL, D_FF), expert, pipeline_mode=resident),
                      pl.BlockSpec((None, D_FF, D_MODEL), expert, pipeline_mode=resident)],
            out_specs=pl.BlockSpec((RB, half), row),
        ),
        out_shape=jax.ShapeDtypeStruct(x_sorted.shape, jnp.uint32),
        compiler_params=_params("arbitrary"),
        name="moe_ffn",
    )(block_expert, n_used, x_sorted, w1, w3, w2)


def _combine_kernel(pos1_ref, pos2_ref, next1_ref, next2_ref, h_ref, meta_ref, g_ref, y_ref, out_ref,
                    y1_s, y2_s, sems):
    half = D_MODEL // 2
    i = pl.program_id(0)
    slot = lax.rem(i, 2)

    def copies(p1_ref, p2_ref, s, t):
        return (_row_copy(y_ref, p1_ref[t], y1_s.at[s], t, sems.at[s]),
                _row_copy(y_ref, p2_ref[t], y2_s.at[s], t, sems.at[s]))

    def start_block(p1_ref, p2_ref, s):
        def start(t, _):
            for c in copies(p1_ref, p2_ref, s, t):
                c.start()
            return 0
        lax.fori_loop(0, MOVE_ROWS, start, 0, unroll=8)

    @pl.when(i == 0)
    def _():
        start_block(pos1_ref, pos2_ref, 0)

    @pl.when(i + 1 < pl.num_programs(0))
    def _():
        start_block(next1_ref, next2_ref, 1 - slot)

    def wait(t, _):
        for c in copies(pos1_ref, pos2_ref, slot, t):
            c.wait()
        return 0

    lax.fori_loop(0, MOVE_ROWS, wait, 0, unroll=8)
    meta = meta_ref[...]
    g1 = meta[:, META_G1:META_G1 + 1]
    g2 = meta[:, META_G2:META_G2 + 1]
    lo1, hi1 = _unpack_halves(y1_s[slot])
    lo2, hi2 = _unpack_halves(y2_s[slot])
    lo = h_ref[:, :half] + g1 * lo1 + g2 * lo2
    hi = h_ref[:, half:] + g1 * hi1 + g2 * hi2
    ms = (jnp.sum(lo * lo, axis=-1, keepdims=True) + jnp.sum(hi * hi, axis=-1, keepdims=True)) * (1.0 / D_MODEL)
    inv = lax.rsqrt(ms + EPS)
    out_ref[:, :half] = lo * inv * g_ref[:, :half]
    out_ref[:, half:] = hi * inv * g_ref[:, half:]


def _combine(y_sorted, h, meta, pos1, pos2, g_final, row0, n_rows):
    half = D_MODEL // 2
    b0 = row0 // MOVE_ROWS
    n_steps = n_rows // MOVE_ROWS
    idx = pl.BlockSpec((MOVE_ROWS,), lambda i: (b0 + i,), memory_space=pltpu.SMEM)
    nxt = pl.BlockSpec((MOVE_ROWS,), lambda i: (b0 + jnp.minimum(i + 1, n_steps - 1),), memory_space=pltpu.SMEM)
    row = lambda i: (b0 + i, 0)
    gathered = pltpu.VMEM((2, MOVE_ROWS, half), jnp.uint32)
    return pl.pallas_call(
        _combine_kernel,
        grid=(n_steps,),
        in_specs=[idx, idx, nxt, nxt, pl.BlockSpec((MOVE_ROWS, D_MODEL), row), pl.BlockSpec((MOVE_ROWS, LANES), row),
                  pl.BlockSpec((1, D_MODEL), lambda i: (0, 0)), pl.BlockSpec(memory_space=pl.ANY)],
        out_specs=pl.BlockSpec((MOVE_ROWS, D_MODEL), lambda i: (i, 0)),
        out_shape=jax.ShapeDtypeStruct((n_rows, D_MODEL), F32),
        scratch_shapes=[gathered, gathered, pltpu.SemaphoreType.DMA((2,))],
        compiler_params=_params("arbitrary"),
        name="moe_combine",
    )(pos1, pos2, pos1, pos2, h, meta, g_final, y_sorted)


def _routing_tables(metat, counts, n_blocks):
    RB = FFN_ROWS
    cnt = counts[0, :N_EXPERTS].astype(jnp.int32)
    padded = (cnt + RB - 1) // RB * RB
    ends = jnp.cumsum(padded)
    starts = ends - padded
    e1 = metat[META_E1].astype(jnp.int32)
    e2 = metat[META_E2].astype(jnp.int32)
    pos1 = starts[e1] + metat[META_R1].astype(jnp.int32)
    pos2 = starts[e2] + metat[META_R2].astype(jnp.int32)
    block_start = jnp.arange(n_blocks, dtype=jnp.int32) * RB
    block_expert = jnp.sum((block_start[:, None] >= ends[None, :]).astype(jnp.int32), axis=1)
    block_expert = jnp.minimum(block_expert, N_EXPERTS - 1)
    n_used = (ends[-1:] // RB).astype(jnp.int32)
    return pos1, pos2, block_expert, n_used, starts + cnt, ends


def _block_diag(w):
    nb, bs, _ = w.shape
    eye = jnp.eye(nb, dtype=w.dtype)
    return jnp.einsum("ncf,nm->ncmf", w, eye).reshape(nb * bs, nb * bs)


def kernel(x_prompt, x_sample, norm_mix, w_in, conv_w, conv_b, w_rgate, b_rgate, w_igate, b_igate,
           lru_lambda, attn_sink, norm_rec_out, norm_attn_out, w_out, norm_ffn, ffn_w1, ffn_w3, ffn_w2,
           moe_router, moe_router_bias, moe_w1, moe_w3, moe_w2, norm_final):
    n_long, s_long, _ = x_prompt.shape
    n_short, s_short, _ = x_sample.shape
    depth = w_in.shape[0]
    assert depth == 2, "layer 0 uses the dense feed-forward, layer 1 the mixture of experts"
    for s in (s_long, s_short):
        assert s % max(ROW_BLOCK, ATT_BLOCK) == 0
    st = Stream(n_long, s_long, n_short, s_short)
    assert st.rows % FFN_ROWS == 0
    assert st.n_long_rows % MOVE_ROWS == 0 and st.rows % MOVE_ROWS == 0

    h = (x_prompt.reshape(-1, D_MODEL), x_sample.reshape(-1, D_MODEL))
    tables = _rope_tables(max(s_long, s_short))
    row2 = lambda a: a.reshape(1, -1)

    for l in range(depth):
        xrec, gate, q, k, v = _inproj(st, h, row2(norm_mix[l]), w_in[l].astype(BF16), tables)
        wg = [(0.5 * jnp.concatenate([_block_diag(w_rgate[l, d]), _block_diag(w_igate[l, d])], axis=1)).astype(BF16)
              for d in range(2)]
        bg = [row2(0.5 * jnp.concatenate([b_rgate[l, d], b_igate[l, d]])) for d in range(2)]
        cw, cb = conv_w[l], row2(conv_b[l])
        hf = _recurrent_fwd(st, xrec, cw, cb, wg[0], bg[0], row2(lru_lambda[l, 0]))
        rec_bwd_args = (xrec, xrec, xrec, cw, cb, wg[1], bg[1], row2(lru_lambda[l, 1]), hf, gate,
                        row2(norm_rec_out[l]))
        yatt = _attention(st, q, k, v, attn_sink[l], row2(norm_attn_out[l]))
        if l % 2 == 0:
            j = l // 2
            h, hn = _outproj(st, h, rec_bwd_args, yatt, w_out[l].astype(BF16), row2(norm_ffn[l]))
            h = _ffn_dense(st, hn, h, ffn_w1[j].astype(BF16), ffn_w3[j].astype(BF16), ffn_w2[j].astype(BF16))
        else:
            j = l // 2
            h, hn_packed, meta, metat, counts = _outproj_router(
                st, h, rec_bwd_args, yatt, w_out[l].astype(BF16), row2(norm_ffn[l]), moe_router[j],
                moe_router_bias[j])
            n_blocks = TOP_K * st.rows // FFN_ROWS + N_EXPERTS
            pos1, pos2, block_expert, n_used, pad_lo, pad_hi = _routing_tables(metat, counts, n_blocks)
            x_sorted = _dispatch(st, hn_packed, pos1, pos2, pad_lo, pad_hi, n_blocks * FFN_ROWS)
            y_sorted = _moe_ffn(x_sorted, block_expert, n_used, moe_w1[j].astype(BF16), moe_w3[j].astype(BF16),
                                moe_w2[j].astype(BF16))
            outs = [_combine(y_sorted, h, meta, pos1, pos2, row2(norm_final), r0, n)
                    for r0, n in ((0, st.n_long_rows), (st.n_long_rows, st.rows - st.n_long_rows))]
    return (outs[0].reshape(x_prompt.shape), outs[1].reshape(x_sample.shape))
```
